```python
import math
import jax, jax.numpy as jnp
from jax import lax
import numpy as np

D_MODEL = 1024
BATCH = 8
SEQ = 2048
DEPTH = 4
DEC_BATCH = 128
DEC_SEQ = 1
PAST_LEN = 16384
PAGE_SIZE = 128

N_MIXERS = 2
N_SSM_LAYERS = (DEPTH + 1) // 2
N_CONV_LAYERS = DEPTH // 2
GROUP_SIZE = 16
N_GROUPS = D_MODEL // GROUP_SIZE
STATE_DIM = 64
CONV_WIDTH = 3
D_FF = 4 * D_MODEL
RMS_EPS = 1e-6
DT_MIN = 1e-3
DT_MAX = 1e-1

kernel_name = "s5_shortconv_hybrid_decode_step"


def rms_norm(x, g):
    xf = x.astype(jnp.float32)
    ms = jnp.mean(xf * xf, axis=-1, keepdims=True)
    return (xf * lax.rsqrt(ms + RMS_EPS) * g.astype(jnp.float32)).astype(x.dtype)


def _ssm_combine(left, right):
    a_l, b_l = left
    a_r, b_r = right
    return a_r * a_l, a_r * b_l + b_r


def s5_mixer(u, h0_re, h0_im, a_re, a_im, log_dt, b_re, b_im, c_re, c_im, d_skip, w_glu):
    bt, s, _ = u.shape
    uf = u.astype(jnp.float32).reshape(bt, s, N_GROUPS, GROUP_SIZE)
    a = lax.complex(a_re.astype(jnp.float32), a_im.astype(jnp.float32))
    dt = jnp.exp(log_dt.astype(jnp.float32))[:, None]
    a_bar = jnp.exp(a * dt)
    b_c = lax.complex(b_re.astype(jnp.float32), b_im.astype(jnp.float32))
    c_c = lax.complex(c_re.astype(jnp.float32), c_im.astype(jnp.float32))
    b_bar = ((a_bar - 1.0) / a)[..., None] * b_c
    bu = jnp.einsum('gph,bsgh->bsgp', b_bar, uf.astype(jnp.complex64))
    h0 = lax.complex(h0_re.astype(jnp.float32), h0_im.astype(jnp.float32))
    bu = bu.at[:, 0].add(a_bar[None] * h0)
    a_seq = jnp.broadcast_to(a_bar, bu.shape)
    _, h = lax.associative_scan(_ssm_combine, (a_seq, bu), axis=1)
    y = jnp.einsum('ghp,bsgp->bsgh', c_c, h).real + d_skip.astype(jnp.float32).reshape(N_GROUPS, GROUP_SIZE) * uf
    y = jax.nn.gelu(y.reshape(bt, s, D_MODEL))
    z = y @ w_glu.astype(jnp.float32)
    out = z[..., :D_MODEL] * jax.nn.sigmoid(z[..., D_MODEL:])
    h_last = h[:, -1]
    return out.astype(u.dtype), h_last.real, h_last.imag


def short_conv_mixer(u, buf, w_in, conv_w, w_out):
    s = u.shape[1]
    bcv = u @ w_in
    gate_b = bcv[..., :D_MODEL]
    gate_c = bcv[..., D_MODEL:2 * D_MODEL]
    v = bcv[..., 2 * D_MODEL:]
    cv = gate_c * v
    xp = jnp.concatenate([buf.astype(cv.dtype), cv], axis=1)
    y = conv_w[0] * xp[:, 0:s]
    for k in range(1, CONV_WIDTH):
        y = y + conv_w[k] * xp[:, k:k + s]
    out = (gate_b * y) @ w_out
    return out, xp[:, -(CONV_WIDTH - 1):]


def squared_relu_mlp(x, w_up, w_down):
    h = jax.nn.relu(x @ w_up)
    return (h * h) @ w_down


def trunk(x, h_re, h_im, conv_buf, norm_mix, norm_mlp, norm_final,
          ssm_a_re, ssm_a_im, ssm_log_dt, ssm_b_re, ssm_b_im, ssm_c_re, ssm_c_im, ssm_d, ssm_w_glu,
          conv_w_in, conv_w, conv_w_out, mlp_w_up, mlp_w_down):
    new_re, new_im, new_buf = [], [], []
    for i in range(DEPTH):
        j = i // N_MIXERS
        hn = rms_norm(x, norm_mix[i])
        if i % N_MIXERS == 0:
            m, hr, hi = s5_mixer(hn, h_re[j], h_im[j], ssm_a_re[j], ssm_a_im[j], ssm_log_dt[j],
                                 ssm_b_re[j], ssm_b_im[j], ssm_c_re[j], ssm_c_im[j], ssm_d[j], ssm_w_glu[j])
            new_re.append(hr)
            new_im.append(hi)
        else:
            m, nb = short_conv_mixer(hn, conv_buf[j], conv_w_in[j], conv_w[j], conv_w_out[j])
            new_buf.append(nb)
        x = x + m
        x = x + squared_relu_mlp(rms_norm(x, norm_mlp[i]), mlp_w_up[i], mlp_w_down[i])
    y = rms_norm(x, norm_final)
    return y, jnp.stack(new_re), jnp.stack(new_im), jnp.stack(new_buf)


def setup_inputs(seed: int = 0) -> dict:
    key = jax.random.key(seed)
    ks = jax.random.split(key, 24)
    f32 = jnp.float32
    n = jnp.arange(STATE_DIM, dtype=f32)
    a_re = -0.5 + 0.01 * jax.random.normal(ks[5], (N_SSM_LAYERS, N_GROUPS, STATE_DIM), f32)
    a_im = math.pi * n + 0.01 * jax.random.normal(ks[6], (N_SSM_LAYERS, N_GROUPS, STATE_DIM), f32)
    log_dt = jax.random.uniform(ks[7], (N_SSM_LAYERS, N_GROUPS), f32, math.log(DT_MIN), math.log(DT_MAX))
    b_scale = (2.0 * GROUP_SIZE) ** -0.5
    c_scale = (2.0 * STATE_DIM) ** -0.5
    return {
        'x_prompt': jax.random.normal(ks[0], (BATCH, SEQ, D_MODEL), f32),
        'x_sample': jax.random.normal(ks[1], (DEC_BATCH, DEC_SEQ, D_MODEL), f32),
        'state_ssm_re': 0.3 * jax.random.normal(ks[2], (N_SSM_LAYERS, DEC_BATCH, N_GROUPS, STATE_DIM), f32),
        'state_ssm_im': 0.3 * jax.random.normal(ks[3], (N_SSM_LAYERS, DEC_BATCH, N_GROUPS, STATE_DIM), f32),
        'state_conv': jax.random.normal(ks[4], (N_CONV_LAYERS, DEC_BATCH, CONV_WIDTH - 1, D_MODEL), f32),
        'norm_mix': 1.0 + 0.02 * jax.random.normal(ks[8], (DEPTH, D_MODEL), f32),
        'norm_mlp': 1.0 + 0.02 * jax.random.normal(ks[9], (DEPTH, D_MODEL), f32),
        'norm_final': 1.0 + 0.02 * jax.random.normal(ks[10], (D_MODEL,), f32),
        'ssm_a_re': a_re,
        'ssm_a_im': a_im,
        'ssm_log_dt': log_dt,
        'ssm_b_re': b_scale * jax.random.normal(ks[11], (N_SSM_LAYERS, N_GROUPS, STATE_DIM, GROUP_SIZE), f32),
        'ssm_b_im': b_scale * jax.random.normal(ks[12], (N_SSM_LAYERS, N_GROUPS, STATE_DIM, GROUP_SIZE), f32),
        'ssm_c_re': c_scale * jax.random.normal(ks[13], (N_SSM_LAYERS, N_GROUPS, GROUP_SIZE, STATE_DIM), f32),
        'ssm_c_im': c_scale * jax.random.normal(ks[14], (N_SSM_LAYERS, N_GROUPS, GROUP_SIZE, STATE_DIM), f32),
        'ssm_d': 1.0 + 0.1 * jax.random.normal(ks[15], (N_SSM_LAYERS, D_MODEL), f32),
        'ssm_w_glu': D_MODEL ** -0.5 * jax.random.normal(ks[16], (N_SSM_LAYERS, D_MODEL, 2 * D_MODEL), f32),
        'conv_w_in': D_MODEL ** -0.5 * jax.random.normal(ks[17], (N_CONV_LAYERS, D_MODEL, 3 * D_MODEL), f32),
        'conv_w': CONV_WIDTH ** -0.5 * jax.random.normal(ks[18], (N_CONV_LAYERS, CONV_WIDTH, D_MODEL), f32),
        'conv_w_out': D_MODEL ** -0.5 * jax.random.normal(ks[19], (N_CONV_LAYERS, D_MODEL, D_MODEL), f32),
        'mlp_w_up': D_MODEL ** -0.5 * jax.random.normal(ks[20], (DEPTH, D_MODEL, D_FF), f32),
        'mlp_w_down': D_FF ** -0.5 * jax.random.normal(ks[21], (DEPTH, D_FF, D_MODEL), f32),
    }


def reference(x_prompt, x_sample, state_ssm_re, state_ssm_im, state_conv, norm_mix, norm_mlp, norm_final,
              ssm_a_re, ssm_a_im, ssm_log_dt, ssm_b_re, ssm_b_im, ssm_c_re, ssm_c_im, ssm_d, ssm_w_glu,
              conv_w_in, conv_w, conv_w_out, mlp_w_up, mlp_w_down):
    zero_re = jnp.zeros((N_SSM_LAYERS, BATCH, N_GROUPS, STATE_DIM), jnp.float32)
    zero_im = jnp.zeros((N_SSM_LAYERS, BATCH, N_GROUPS, STATE_DIM), jnp.float32)
    zero_buf = jnp.zeros((N_CONV_LAYERS, BATCH, CONV_WIDTH - 1, D_MODEL), x_prompt.dtype)
    y_prompt, new_ssm_re_prompt, new_ssm_im_prompt, new_conv_prompt = trunk(
        x_prompt, zero_re, zero_im, zero_buf, norm_mix, norm_mlp, norm_final,
        ssm_a_re, ssm_a_im, ssm_log_dt, ssm_b_re, ssm_b_im, ssm_c_re, ssm_c_im, ssm_d, ssm_w_glu,
        conv_w_in, conv_w, conv_w_out, mlp_w_up, mlp_w_down)
    y_sample, new_ssm_re_sample, new_ssm_im_sample, new_conv_sample = trunk(
        x_sample, state_ssm_re, state_ssm_im, state_conv, norm_mix, norm_mlp, norm_final,
        ssm_a_re, ssm_a_im, ssm_log_dt, ssm_b_re, ssm_b_im, ssm_c_re, ssm_c_im, ssm_d, ssm_w_glu,
        conv_w_in, conv_w, conv_w_out, mlp_w_up, mlp_w_down)
    return (y_prompt, y_sample, new_ssm_re_prompt, new_ssm_im_prompt, new_conv_prompt,
            new_ssm_re_sample, new_ssm_im_sample, new_conv_sample)
```

```python
import functools
import math

import jax
import jax.numpy as jnp
from jax import lax
from jax.experimental import pallas as pl
from jax.experimental.pallas import tpu as pltpu

D_MODEL = 1024
DEPTH = 4
GROUP_SIZE = 16
N_GROUPS = D_MODEL // GROUP_SIZE
STATE_DIM = 64
N_STATE = N_GROUPS * STATE_DIM
CONV_WIDTH = 3
D_FF = 4 * D_MODEL
RMS_EPS = 1e-6

MXU_K = 256
N_SLABS = D_MODEL // MXU_K
SLAB_STATE = N_STATE // N_SLABS
FF_CHUNK = 1024
VMEM_LIMIT = 56 * 1024 * 1024
ROW_TILE = 512


def _rms_norm(x, g):
    ms = jnp.mean(x * x, axis=-1, keepdims=True)
    return x * lax.rsqrt(ms + RMS_EPS) * g


def _const_spec(shape):
    return pl.BlockSpec(shape, lambda i: (0,) * len(shape), pipeline_mode=pl.Buffered(1))


def _s5_kernel(x_ref, h0re_ref, h0im_ref, g_ref, are_ref, aim_ref, wb_ref, wc_ref, d_ref, wglu_ref,
               o_ref, hre_ref, him_ref, bu_ref, hb_ref, y_ref, *, rows_per_step, lane_width):
    br = rows_per_step
    m = x_ref.shape[0]
    n_steps = m // br
    steps_per_iter = 2 if (n_steps % 2 == 0 and br < 16) else 1

    @pl.when(pl.program_id(0) == 0)
    def _():
        hre_ref[...] = h0re_ref[...]
        him_ref[...] = h0im_ref[...]

    x = x_ref[...]
    u = _rms_norm(x, g_ref[...])
    ub = u.astype(jnp.bfloat16)

    for s in range(N_SLABS):
        ch = slice(s * MXU_K, (s + 1) * MXU_K)
        bu_ref[...] = jnp.dot(ub[:, ch], wb_ref[s], preferred_element_type=jnp.float32)

        for l0 in range(0, SLAB_STATE, lane_width):
            re_l = slice(l0, l0 + lane_width)
            im_l = slice(SLAB_STATE + l0, SLAB_STATE + l0 + lane_width)
            st_l = slice(s * SLAB_STATE + l0, s * SLAB_STATE + l0 + lane_width)
            a_r = are_ref[:, st_l]
            a_i = aim_ref[:, st_l]
            if br <= 8:
                a_r = jnp.broadcast_to(a_r, (br, lane_width))
                a_i = jnp.broadcast_to(a_i, (br, lane_width))

            def body(it, carry, re_l=re_l, im_l=im_l, a_r=a_r, a_i=a_i):
                h_r, h_i = carry
                r0 = pl.multiple_of(it * (steps_per_iter * br), steps_per_iter * br)
                out_r, out_i = [], []
                for k in range(steps_per_iter):
                    b_r = bu_ref[pl.ds(r0 + k * br, br), re_l]
                    b_i = bu_ref[pl.ds(r0 + k * br, br), im_l]
                    n_r = a_r * h_r - a_i * h_i + b_r
                    n_i = a_r * h_i + a_i * h_r + b_i
                    h_r, h_i = n_r, n_i
                    out_r.append(n_r)
                    out_i.append(n_i)
                rows = pl.ds(r0, steps_per_iter * br)
                hb_ref[rows, re_l] = jnp.concatenate(out_r, axis=0).astype(jnp.bfloat16)
                hb_ref[rows, im_l] = jnp.concatenate(out_i, axis=0).astype(jnp.bfloat16)
                return h_r, h_i

            n_iter = n_steps // steps_per_iter
            h_r, h_i = lax.fori_loop(0, n_iter, body, (hre_ref[:, st_l], him_ref[:, st_l]),
                                     unroll=min(4, n_iter))
            hre_ref[:, st_l] = h_r
            him_ref[:, st_l] = h_i

        ys = jnp.dot(hb_ref[...], wc_ref[s], preferred_element_type=jnp.float32)
        ys = ys + d_ref[:, ch] * u[:, ch]
        y_ref[:, ch] = jax.nn.gelu(ys).astype(jnp.bfloat16)

    z = jnp.dot(y_ref[...], wglu_ref[...], preferred_element_type=jnp.float32)
    o_ref[...] = x + z[:, :D_MODEL] * jax.nn.sigmoid(z[:, D_MODEL:])


def _s5_mixer(x, h0_re, h0_im, gain, a_re, a_im, wb, wc, d_skip, w_glu, *, rows_per_step):
    n_rows = x.shape[0]
    br = rows_per_step
    m = min(ROW_TILE, n_rows)
    lane_width = max(128, min(512, 4096 // br))
    kern = functools.partial(_s5_kernel, rows_per_step=br, lane_width=lane_width)
    return pl.pallas_call(
        kern,
        grid=(n_rows // m,),
        in_specs=[
            pl.BlockSpec((m, D_MODEL), lambda i: (i, 0)),
            _const_spec((br, N_STATE)), _const_spec((br, N_STATE)),
            _const_spec((1, D_MODEL)),
            _const_spec((1, N_STATE)), _const_spec((1, N_STATE)),
            _const_spec((N_SLABS, MXU_K, 2 * SLAB_STATE)),
            _const_spec((N_SLABS, 2 * SLAB_STATE, MXU_K)),
            _const_spec((1, D_MODEL)),
            _const_spec((D_MODEL, 2 * D_MODEL)),
        ],
        out_specs=[
            pl.BlockSpec((m, D_MODEL), lambda i: (i, 0)),
            pl.BlockSpec((br, N_STATE), lambda i: (0, 0)),
            pl.BlockSpec((br, N_STATE), lambda i: (0, 0)),
        ],
        out_shape=[
            jax.ShapeDtypeStruct((n_rows, D_MODEL), jnp.float32),
            jax.ShapeDtypeStruct((br, N_STATE), jnp.float32),
            jax.ShapeDtypeStruct((br, N_STATE), jnp.float32),
        ],
        scratch_shapes=[
            pltpu.VMEM((m, 2 * SLAB_STATE), jnp.float32),
            pltpu.VMEM((m, 2 * SLAB_STATE), jnp.bfloat16),
            pltpu.VMEM((m, D_MODEL), jnp.bfloat16),
        ],
        compiler_params=pltpu.CompilerParams(dimension_semantics=("arbitrary",),
                                             vmem_limit_bytes=VMEM_LIMIT),
        name="s5_mixer",
    )(x, h0_re, h0_im, gain, a_re, a_im, wb, wc, d_skip, w_glu)


def _conv_kernel(x_ref, buf_ref, g_ref, win_ref, cw_ref, wout_ref, o_ref, nbuf_ref, cvx_ref,
                 *, rows_per_step):
    br = rows_per_step
    m = x_ref.shape[0]
    hist = (CONV_WIDTH - 1) * br

    @pl.when(pl.program_id(0) == 0)
    def _():
        cvx_ref[0:hist, :] = buf_ref[...]

    x = x_ref[...]
    ub = _rms_norm(x, g_ref[...]).astype(jnp.bfloat16)
    bcv = jnp.dot(ub, win_ref[...], preferred_element_type=jnp.float32)
    cv = bcv[:, D_MODEL:2 * D_MODEL] * bcv[:, 2 * D_MODEL:]
    cvx_ref[hist:hist + m, :] = cv
    y = cw_ref[0:1, :] * cvx_ref[0:m, :]
    for k in range(1, CONV_WIDTH - 1):
        y = y + cw_ref[k:k + 1, :] * cvx_ref[k * br:k * br + m, :]
    y = y + cw_ref[CONV_WIDTH - 1:CONV_WIDTH, :] * cv
    gated = (bcv[:, :D_MODEL] * y).astype(jnp.bfloat16)
    o_ref[...] = x + jnp.dot(gated, wout_ref[...], preferred_element_type=jnp.float32)
    tail = cvx_ref[m:m + hist, :]
    cvx_ref[0:hist, :] = tail
    nbuf_ref[...] = tail


def _conv_mixer(x, buf, gain, w_in, conv_w, w_out, *, rows_per_step):
    n_rows = x.shape[0]
    br = rows_per_step
    hist = (CONV_WIDTH - 1) * br
    m = min(ROW_TILE, n_rows)
    kern = functools.partial(_conv_kernel, rows_per_step=br)
    return pl.pallas_call(
        kern,
        grid=(n_rows // m,),
        in_specs=[
            pl.BlockSpec((m, D_MODEL), lambda i: (i, 0)),
            _const_spec((hist, D_MODEL)),
            _const_spec((1, D_MODEL)),
            _const_spec((D_MODEL, 3 * D_MODEL)),
            _const_spec((CONV_WIDTH, D_MODEL)),
            _const_spec((D_MODEL, D_MODEL)),
        ],
        out_specs=[
            pl.BlockSpec((m, D_MODEL), lambda i: (i, 0)),
            pl.BlockSpec((hist, D_MODEL), lambda i: (0, 0)),
        ],
        out_shape=[
            jax.ShapeDtypeStruct((n_rows, D_MODEL), jnp.float32),
            jax.ShapeDtypeStruct((hist, D_MODEL), jnp.float32),
        ],
        scratch_shapes=[pltpu.VMEM((hist + m, D_MODEL), jnp.float32)],
        compiler_params=pltpu.CompilerParams(dimension_semantics=("arbitrary",),
                                             vmem_limit_bytes=VMEM_LIMIT),
        name="conv_mixer",
    )(x, buf, gain, w_in, conv_w, w_out)


def _mlp_kernel(x_ref, g_ref, up_ref, down_ref, gf_ref, o_ref, *, final_norm):
    x = x_ref[...]
    ub = _rms_norm(x, g_ref[...]).astype(jnp.bfloat16)
    acc = x
    for c in range(0, D_FF, FF_CHUNK):
        h = jnp.dot(ub, up_ref[:, c:c + FF_CHUNK], preferred_element_type=jnp.float32)
        h = jnp.maximum(h, 0.0)
        h = (h * h).astype(jnp.bfloat16)
        acc = acc + jnp.dot(h, down_ref[c:c + FF_CHUNK, :], preferred_element_type=jnp.float32)
    if final_norm:
        acc = _rms_norm(acc, gf_ref[...])
    o_ref[...] = acc


def _mlp(x, gain, w_up, w_down, gain_final, *, final_norm):
    n_rows = x.shape[0]
    m = min(ROW_TILE, n_rows)
    kern = functools.partial(_mlp_kernel, final_norm=final_norm)
    return pl.pallas_call(
        kern,
        grid=(n_rows // m,),
        in_specs=[
            pl.BlockSpec((m, D_MODEL), lambda i: (i, 0)),
            _const_spec((1, D_MODEL)),
            _const_spec((D_MODEL, D_FF)),
            _const_spec((D_FF, D_MODEL)),
            _const_spec((1, D_MODEL)),
        ],
        out_specs=pl.BlockSpec((m, D_MODEL), lambda i: (i, 0)),
        out_shape=jax.ShapeDtypeStruct((n_rows, D_MODEL), jnp.float32),
        compiler_params=pltpu.CompilerParams(dimension_semantics=("arbitrary",),
                                             vmem_limit_bytes=VMEM_LIMIT),
        name="mlp",
    )(x, gain, w_up, w_down, gain_final)


def _s5_params(a_re, a_im, log_dt, b_re, b_im, c_re, c_im):
    dt = jnp.exp(log_dt)[:, None]
    mag = jnp.exp(a_re * dt)
    ab_re = mag * jnp.cos(a_im * dt)
    ab_im = mag * jnp.sin(a_im * dt)
    den = a_re * a_re + a_im * a_im
    q_re = ((ab_re - 1.0) * a_re + ab_im * a_im) / den
    q_im = (ab_im * a_re - (ab_re - 1.0) * a_im) / den
    bb_re = q_re[..., None] * b_re - q_im[..., None] * b_im
    bb_im = q_re[..., None] * b_im + q_im[..., None] * b_re
    gps = MXU_K // GROUP_SIZE
    eye = jnp.eye(gps, dtype=jnp.float32)

    def pack_b(w):
        w = w.reshape(N_SLABS, gps, STATE_DIM, GROUP_SIZE)
        return jnp.einsum('sgph,gk->sghkp', w, eye).reshape(N_SLABS, MXU_K, SLAB_STATE)

    def pack_c(w):
        w = w.reshape(N_SLABS, gps, GROUP_SIZE, STATE_DIM)
        return jnp.einsum('sghp,gk->sgpkh', w, eye).reshape(N_SLABS, SLAB_STATE, MXU_K)

    wb = jnp.concatenate([pack_b(bb_re), pack_b(bb_im)], axis=2).astype(jnp.bfloat16)
    wc = jnp.concatenate([pack_c(c_re), pack_c(-c_im)], axis=1).astype(jnp.bfloat16)
    return ab_re.reshape(1, N_STATE), ab_im.reshape(1, N_STATE), wb, wc


def _trunk(x, h_re, h_im, conv_buf, p, *, rows_per_step):
    new_re, new_im, new_buf = [], [], []
    for i in range(DEPTH):
        j = i // 2
        if i % 2 == 0:
            x, hr, hi = _s5_mixer(x, h_re[j], h_im[j], p['norm_mix'][i], *p['s5'][j],
                                  p['ssm_d'][j], p['w_glu'][j], rows_per_step=rows_per_step)
            new_re.append(hr)
            new_im.append(hi)
        else:
            x, nb = _conv_mixer(x, conv_buf[j], p['norm_mix'][i], p['w_in'][j], p['conv_w'][j],
                                p['w_out'][j], rows_per_step=rows_per_step)
            new_buf.append(nb)
        x = _mlp(x, p['norm_mlp'][i], p['w_up'][i], p['w_down'][i], p['norm_final'],
                 final_norm=(i == DEPTH - 1))
    return x, new_re, new_im, new_buf


def kernel(x_prompt, x_sample, state_ssm_re, state_ssm_im, state_conv, norm_mix, norm_mlp, norm_final,
           ssm_a_re, ssm_a_im, ssm_log_dt, ssm_b_re, ssm_b_im, ssm_c_re, ssm_c_im, ssm_d, ssm_w_glu,
           conv_w_in, conv_w, conv_w_out, mlp_w_up, mlp_w_down):
    f32, bf16 = jnp.float32, jnp.bfloat16
    batch, seq, _ = x_prompt.shape
    dec_batch = x_sample.shape[0]
    n_ssm = ssm_a_re.shape[0]
    n_conv = conv_w_in.shape[0]
    hist = CONV_WIDTH - 1

    p = {
        'norm_mix': norm_mix.reshape(DEPTH, 1, D_MODEL),
        'norm_mlp': norm_mlp.reshape(DEPTH, 1, D_MODEL),
        'norm_final': norm_final.reshape(1, D_MODEL),
        's5': [_s5_params(ssm_a_re[j], ssm_a_im[j], ssm_log_dt[j], ssm_b_re[j], ssm_b_im[j],
                          ssm_c_re[j], ssm_c_im[j]) for j in range(n_ssm)],
        'ssm_d': ssm_d.reshape(n_ssm, 1, D_MODEL),
        'w_glu': ssm_w_glu.astype(bf16),
        'w_in': conv_w_in.astype(bf16),
        'conv_w': conv_w,
        'w_out': conv_w_out.astype(bf16),
        'w_up': mlp_w_up.astype(bf16),
        'w_down': mlp_w_down.astype(bf16),
    }

    xp = jnp.transpose(x_prompt, (1, 0, 2)).reshape(seq * batch, D_MODEL)
    zero_h = jnp.zeros((n_ssm, batch, N_STATE), f32)
    zero_buf = jnp.zeros((n_conv, hist * batch, D_MODEL), f32)
    yp, pre, pim, pbuf = _trunk(xp, zero_h, zero_h, zero_buf, p, rows_per_step=batch)
    y_prompt = jnp.transpose(yp.reshape(seq, batch, D_MODEL), (1, 0, 2))
    new_re_p = jnp.stack(pre).reshape(n_ssm, batch, N_GROUPS, STATE_DIM)
    new_im_p = jnp.stack(pim).reshape(n_ssm, batch, N_GROUPS, STATE_DIM)
    new_buf_p = jnp.transpose(jnp.stack(pbuf).reshape(n_conv, hist, batch, D_MODEL), (0, 2, 1, 3))

    xs = x_sample.reshape(dec_batch, D_MODEL)
    s_re = state_ssm_re.reshape(n_ssm, dec_batch, N_STATE)
    s_im = state_ssm_im.reshape(n_ssm, dec_batch, N_STATE)
    s_buf = jnp.transpose(state_conv, (0, 2, 1, 3)).reshape(n_conv, hist * dec_batch, D_MODEL)
    ys, sre, sim, sbuf = _trunk(xs, s_re, s_im, s_buf, p, rows_per_step=dec_batch)
    y_sample = ys.reshape(dec_batch, 1, D_MODEL)
    new_re_s = jnp.stack(sre).reshape(n_ssm, dec_batch, N_GROUPS, STATE_DIM)
    new_im_s = jnp.stack(sim).reshape(n_ssm, dec_batch, N_GROUPS, STATE_DIM)
    new_buf_s = jnp.transpose(jnp.stack(sbuf).reshape(n_conv, hist, dec_batch, D_MODEL), (0, 2, 1, 3))

    return (y_prompt, y_sample, new_re_p, new_im_p, new_buf_p, new_re_s, new_im_s, new_buf_s)
```

```python
import functools
import math

import jax
import jax.numpy as jnp
from jax import lax
from jax.experimental import pallas as pl
from jax.experimental.pallas import tpu as pltpu

D_MODEL = 1024
DEPTH = 4
GROUP_SIZE = 16
N_GROUPS = D_MODEL // GROUP_SIZE
STATE_DIM = 64
N_STATE = N_GROUPS * STATE_DIM
CONV_WIDTH = 3
D_FF = 4 * D_MODEL
RMS_EPS = 1e-6

MXU_K = 256
N_SLABS = D_MODEL // MXU_K
SLAB_STATE = N_STATE // N_SLABS
FF_CHUNK = 1024
VMEM_LIMIT = 56 * 1024 * 1024
ROW_TILE = 512


def _rms_norm(x, g):
    ms = jnp.mean(x * x, axis=-1, keepdims=True)
    return x * lax.rsqrt(ms + RMS_EPS) * g


def _const_spec(shape):
    return pl.BlockSpec(shape, lambda i: (0,) * len(shape), pipeline_mode=pl.Buffered(1))


def _s5_kernel(x_ref, h0re_ref, h0im_ref, g_ref, are_ref, aim_ref, wb_ref, wc_ref, d_ref, wglu_ref,
               o_ref, hre_ref, him_ref, bu_ref, hb_ref, y_ref, *, rows_per_step, lane_width):
    br = rows_per_step
    m = x_ref.shape[0]
    n_steps = m // br
    steps_per_iter = 2 if (n_steps % 2 == 0 and br < 16) else 1

    @pl.when(pl.program_id(0) == 0)
    def _():
        hre_ref[...] = h0re_ref[...]
        him_ref[...] = h0im_ref[...]

    x = x_ref[...]
    u = _rms_norm(x, g_ref[...])
    ub = u.astype(jnp.bfloat16)

    for s in range(N_SLABS):
        ch = slice(s * MXU_K, (s + 1) * MXU_K)
        bu_ref[...] = jnp.dot(ub[:, ch], wb_ref[s], preferred_element_type=jnp.float32)

        for l0 in range(0, SLAB_STATE, lane_width):
            re_l = slice(l0, l0 + lane_width)
            im_l = slice(SLAB_STATE + l0, SLAB_STATE + l0 + lane_width)
            st_l = slice(s * SLAB_STATE + l0, s * SLAB_STATE + l0 + lane_width)
            a_r = are_ref[:, st_l]
            a_i = aim_ref[:, st_l]
            if br <= 8:
                a_r = jnp.broadcast_to(a_r, (br, lane_width))
                a_i = jnp.broadcast_to(a_i, (br, lane_width))

            def body(it, carry, re_l=re_l, im_l=im_l, a_r=a_r, a_i=a_i):
                h_r, h_i = carry
                r0 = pl.multiple_of(it * (steps_per_iter * br), steps_per_iter * br)
                out_r, out_i = [], []
                for k in range(steps_per_iter):
                    b_r = bu_ref[pl.ds(r0 + k * br, br), re_l]
                    b_i = bu_ref[pl.ds(r0 + k * br, br), im_l]
                    n_r = a_r * h_r - a_i * h_i + b_r
                    n_i = a_r * h_i + a_i * h_r + b_i
                    h_r, h_i = n_r, n_i
                    out_r.append(n_r)
                    out_i.append(n_i)
                rows = pl.ds(r0, steps_per_iter * br)
                hb_ref[rows, re_l] = jnp.concatenate(out_r, axis=0).astype(jnp.bfloat16)
                hb_ref[rows, im_l] = jnp.concatenate(out_i, axis=0).astype(jnp.bfloat16)
                return h_r, h_i

            n_iter = n_steps // steps_per_iter
            h_r, h_i = lax.fori_loop(0, n_iter, body, (hre_ref[:, st_l], him_ref[:, st_l]),
                                     unroll=True)
            hre_ref[:, st_l] = h_r
            him_ref[:, st_l] = h_i

        ys = jnp.dot(hb_ref[...], wc_ref[s], preferred_element_type=jnp.float32)
        ys = ys + d_ref[:, ch] * u[:, ch]
        y_ref[:, ch] = jax.nn.gelu(ys).astype(jnp.bfloat16)

    z = jnp.dot(y_ref[...], wglu_ref[...], preferred_element_type=jnp.float32)
    o_ref[...] = x + z[:, :D_MODEL] * jax.nn.sigmoid(z[:, D_MODEL:])


def _s5_mixer(x, h0_re, h0_im, gain, a_re, a_im, wb, wc, d_skip, w_glu, *, rows_per_step):
    n_rows = x.shape[0]
    br = rows_per_step
    m = min(ROW_TILE, n_rows)
    lane_width = max(128, min(512, 4096 // br))
    kern = functools.partial(_s5_kernel, rows_per_step=br, lane_width=lane_width)
    return pl.pallas_call(
        kern,
        grid=(n_rows // m,),
        in_specs=[
            pl.BlockSpec((m, D_MODEL), lambda i: (i, 0)),
            _const_spec((br, N_STATE)), _const_spec((br, N_STATE)),
            _const_spec((1, D_MODEL)),
            _const_spec((1, N_STATE)), _const_spec((1, N_STATE)),
            _const_spec((N_SLABS, MXU_K, 2 * SLAB_STATE)),
            _const_spec((N_SLABS, 2 * SLAB_STATE, MXU_K)),
            _const_spec((1, D_MODEL)),
            _const_spec((D_MODEL, 2 * D_MODEL)),
        ],
        out_specs=[
            pl.BlockSpec((m, D_MODEL), lambda i: (i, 0)),
            pl.BlockSpec((br, N_STATE), lambda i: (0, 0)),
            pl.BlockSpec((br, N_STATE), lambda i: (0, 0)),
        ],
        out_shape=[
            jax.ShapeDtypeStruct((n_rows, D_MODEL), jnp.float32),
            jax.ShapeDtypeStruct((br, N_STATE), jnp.float32),
            jax.ShapeDtypeStruct((br, N_STATE), jnp.float32),
        ],
        scratch_shapes=[
            pltpu.VMEM((m, 2 * SLAB_STATE), jnp.float32),
            pltpu.VMEM((m, 2 * SLAB_STATE), jnp.bfloat16),
            pltpu.VMEM((m, D_MODEL), jnp.bfloat16),
        ],
        compiler_params=pltpu.CompilerParams(dimension_semantics=("arbitrary",),
                                             vmem_limit_bytes=VMEM_LIMIT),
        name="s5_mixer",
    )(x, h0_re, h0_im, gain, a_re, a_im, wb, wc, d_skip, w_glu)


def _conv_kernel(x_ref, buf_ref, g_ref, win_ref, cw_ref, wout_ref, o_ref, nbuf_ref, cvx_ref,
                 *, rows_per_step):
    br = rows_per_step
    m = x_ref.shape[0]
    hist = (CONV_WIDTH - 1) * br

    @pl.when(pl.program_id(0) == 0)
    def _():
        cvx_ref[0:hist, :] = buf_ref[...]

    x = x_ref[...]
    ub = _rms_norm(x, g_ref[...]).astype(jnp.bfloat16)
    bcv = jnp.dot(ub, win_ref[...], preferred_element_type=jnp.float32)
    cv = bcv[:, D_MODEL:2 * D_MODEL] * bcv[:, 2 * D_MODEL:]
    cvx_ref[hist:hist + m, :] = cv
    y = cw_ref[0:1, :] * cvx_ref[0:m, :]
    for k in range(1, CONV_WIDTH - 1):
        y = y + cw_ref[k:k + 1, :] * cvx_ref[k * br:k * br + m, :]
    y = y + cw_ref[CONV_WIDTH - 1:CONV_WIDTH, :] * cv
    gated = (bcv[:, :D_MODEL] * y).astype(jnp.bfloat16)
    o_ref[...] = x + jnp.dot(gated, wout_ref[...], preferred_element_type=jnp.float32)
    tail = cvx_ref[m:m + hist, :]
    cvx_ref[0:hist, :] = tail
    nbuf_ref[...] = tail


def _conv_mixer(x, buf, gain, w_in, conv_w, w_out, *, rows_per_step):
    n_rows = x.shape[0]
    br = rows_per_step
    hist = (CONV_WIDTH - 1) * br
    m = min(ROW_TILE, n_rows)
    kern = functools.partial(_conv_kernel, rows_per_step=br)
    return pl.pallas_call(
        kern,
        grid=(n_rows // m,),
        in_specs=[
            pl.BlockSpec((m, D_MODEL), lambda i: (i, 0)),
            _const_spec((hist, D_MODEL)),
            _const_spec((1, D_MODEL)),
            _const_spec((D_MODEL, 3 * D_MODEL)),
            _const_spec((CONV_WIDTH, D_MODEL)),
            _const_spec((D_MODEL, D_MODEL)),
        ],
        out_specs=[
            pl.BlockSpec((m, D_MODEL), lambda i: (i, 0)),
            pl.BlockSpec((hist, D_MODEL), lambda i: (0, 0)),
        ],
        out_shape=[
            jax.ShapeDtypeStruct((n_rows, D_MODEL), jnp.float32),
            jax.ShapeDtypeStruct((hist, D_MODEL), jnp.float32),
        ],
        scratch_shapes=[pltpu.VMEM((hist + m, D_MODEL), jnp.float32)],
        compiler_params=pltpu.CompilerParams(dimension_semantics=("arbitrary",),
                                             vmem_limit_bytes=VMEM_LIMIT),
        name="conv_mixer",
    )(x, buf, gain, w_in, conv_w, w_out)


def _mlp_kernel(x_ref, g_ref, up_ref, down_ref, gf_ref, o_ref, *, final_norm):
    x = x_ref[...]
    ub = _rms_norm(x, g_ref[...]).astype(jnp.bfloat16)
    acc = x
    for c in range(0, D_FF, FF_CHUNK):
        h = jnp.dot(ub, up_ref[:, c:c + FF_CHUNK], preferred_element_type=jnp.float32)
        h = jnp.maximum(h, 0.0)
        h = (h * h).astype(jnp.bfloat16)
        acc = acc + jnp.dot(h, down_ref[c:c + FF_CHUNK, :], preferred_element_type=jnp.float32)
    if final_norm:
        acc = _rms_norm(acc, gf_ref[...])
    o_ref[...] = acc


def _mlp(x, gain, w_up, w_down, gain_final, *, final_norm):
    n_rows = x.shape[0]
    m = min(ROW_TILE, n_rows)
    kern = functools.partial(_mlp_kernel, final_norm=final_norm)
    return pl.pallas_call(
        kern,
        grid=(n_rows // m,),
        in_specs=[
            pl.BlockSpec((m, D_MODEL), lambda i: (i, 0)),
            _const_spec((1, D_MODEL)),
            _const_spec((D_MODEL, D_FF)),
            _const_spec((D_FF, D_MODEL)),
            _const_spec((1, D_MODEL)),
        ],
        out_specs=pl.BlockSpec((m, D_MODEL), lambda i: (i, 0)),
        out_shape=jax.ShapeDtypeStruct((n_rows, D_MODEL), jnp.float32),
        compiler_params=pltpu.CompilerParams(dimension_semantics=("arbitrary",),
                                             vmem_limit_bytes=VMEM_LIMIT),
        name="mlp",
    )(x, gain, w_up, w_down, gain_final)


def _s5_params(a_re, a_im, log_dt, b_re, b_im, c_re, c_im):
    dt = jnp.exp(log_dt)[:, None]
    mag = jnp.exp(a_re * dt)
    ab_re = mag * jnp.cos(a_im * dt)
    ab_im = mag * jnp.sin(a_im * dt)
    den = a_re * a_re + a_im * a_im
    q_re = ((ab_re - 1.0) * a_re + ab_im * a_im) / den
    q_im = (ab_im * a_re - (ab_re - 1.0) * a_im) / den
    bb_re = q_re[..., None] * b_re - q_im[..., None] * b_im
    bb_im = q_re[..., None] * b_im + q_im[..., None] * b_re
    gps = MXU_K // GROUP_SIZE
    eye = jnp.eye(gps, dtype=jnp.float32)

    def pack_b(w):
        w = w.reshape(N_SLABS, gps, STATE_DIM, GROUP_SIZE)
        return jnp.einsum('sgph,gk->sghkp', w, eye).reshape(N_SLABS, MXU_K, SLAB_STATE)

    def pack_c(w):
        w = w.reshape(N_SLABS, gps, GROUP_SIZE, STATE_DIM)
        return jnp.einsum('sghp,gk->sgpkh', w, eye).reshape(N_SLABS, SLAB_STATE, MXU_K)

    wb = jnp.concatenate([pack_b(bb_re), pack_b(bb_im)], axis=2).astype(jnp.bfloat16)
    wc = jnp.concatenate([pack_c(c_re), pack_c(-c_im)], axis=1).astype(jnp.bfloat16)
    return ab_re.reshape(1, N_STATE), ab_im.reshape(1, N_STATE), wb, wc


def _trunk(x, h_re, h_im, conv_buf, p, *, rows_per_step):
    new_re, new_im, new_buf = [], [], []
    for i in range(DEPTH):
        j = i // 2
        if i % 2 == 0:
            x, hr, hi = _s5_mixer(x, h_re[j], h_im[j], p['norm_mix'][i], *p['s5'][j],
                                  p['ssm_d'][j], p['w_glu'][j], rows_per_step=rows_per_step)
            new_re.append(hr)
            new_im.append(hi)
        else:
            x, nb = _conv_mixer(x, conv_buf[j], p['norm_mix'][i], p['w_in'][j], p['conv_w'][j],
                                p['w_out'][j], rows_per_step=rows_per_step)
            new_buf.append(nb)
        x = _mlp(x, p['norm_mlp'][i], p['w_up'][i], p['w_down'][i], p['norm_final'],
                 final_norm=(i == DEPTH - 1))
    return x, new_re, new_im, new_buf


def kernel(x_prompt, x_sample, state_ssm_re, state_ssm_im, state_conv, norm_mix, norm_mlp, norm_final,
           ssm_a_re, ssm_a_im, ssm_log_dt, ssm_b_re, ssm_b_im, ssm_c_re, ssm_c_im, ssm_d, ssm_w_glu,
           conv_w_in, conv_w, conv_w_out, mlp_w_up, mlp_w_down):
    f32, bf16 = jnp.float32, jnp.bfloat16
    batch, seq, _ = x_prompt.shape
    dec_batch = x_sample.shape[0]
    n_ssm = ssm_a_re.shape[0]
    n_conv = conv_w_in.shape[0]
    hist = CONV_WIDTH - 1

    p = {
        'norm_mix': norm_mix.reshape(DEPTH, 1, D_MODEL),
        'norm_mlp': norm_mlp.reshape(DEPTH, 1, D_MODEL),
        'norm_final': norm_final.reshape(1, D_MODEL),
        's5': [_s5_params(ssm_a_re[j], ssm_a_im[j], ssm_log_dt[j], ssm_b_re[j], ssm_b_im[j],
                          ssm_c_re[j], ssm_c_im[j]) for j in range(n_ssm)],
        'ssm_d': ssm_d.reshape(n_ssm, 1, D_MODEL),
        'w_glu': ssm_w_glu.astype(bf16),
        'w_in': conv_w_in.astype(bf16),
        'conv_w': conv_w,
        'w_out': conv_w_out.astype(bf16),
        'w_up': mlp_w_up.astype(bf16),
        'w_down': mlp_w_down.astype(bf16),
    }

    xp = jnp.transpose(x_prompt, (1, 0, 2)).reshape(seq * batch, D_MODEL)
    zero_h = jnp.zeros((n_ssm, batch, N_STATE), f32)
    zero_buf = jnp.zeros((n_conv, hist * batch, D_MODEL), f32)
    yp, pre, pim, pbuf = _trunk(xp, zero_h, zero_h, zero_buf, p, rows_per_step=batch)
    y_prompt = jnp.transpose(yp.reshape(seq, batch, D_MODEL), (1, 0, 2))
    new_re_p = jnp.stack(pre).reshape(n_ssm, batch, N_GROUPS, STATE_DIM)
    new_im_p = jnp.stack(pim).reshape(n_ssm, batch, N_GROUPS, STATE_DIM)
    new_buf_p = jnp.transpose(jnp.stack(pbuf).reshape(n_conv, hist, batch, D_MODEL), (0, 2, 1, 3))

    xs = x_sample.reshape(dec_batch, D_MODEL)
    s_re = state_ssm_re.reshape(n_ssm, dec_batch, N_STATE)
    s_im = state_ssm_im.reshape(n_ssm, dec_batch, N_STATE)
    s_buf = jnp.transpose(state_conv, (0, 2, 1, 3)).reshape(n_conv, hist * dec_batch, D_MODEL)
    ys, sre, sim, sbuf = _trunk(xs, s_re, s_im, s_buf, p, rows_per_step=dec_batch)
    y_sample = ys.reshape(dec_batch, 1, D_MODEL)
    new_re_s = jnp.stack(sre).reshape(n_ssm, dec_batch, N_GROUPS, STATE_DIM)
    new_im_s = jnp.stack(sim).reshape(n_ssm, dec_batch, N_GROUPS, STATE_DIM)
    new_buf_s = jnp.transpose(jnp.stack(sbuf).reshape(n_conv, hist, dec_batch, D_MODEL), (0, 2, 1, 3))

    return (y_prompt, y_sample, new_re_p, new_im_p, new_buf_p, new_re_s, new_im_s, new_buf_s)
```

```python
import functools

import jax
import jax.numpy as jnp
from jax import lax
from jax.experimental import pallas as pl
from jax.experimental.pallas import tpu as pltpu

D_MODEL = 1024
DEPTH = 4
GROUP_SIZE = 16
N_GROUPS = D_MODEL // GROUP_SIZE
STATE_DIM = 64
N_STATE = N_GROUPS * STATE_DIM
CONV_WIDTH = 3
HIST = CONV_WIDTH - 1
D_FF = 4 * D_MODEL
RMS_EPS = 1e-6

LANES = 128
SUBLANES = 8
MXU_K = 256
N_SLABS = D_MODEL // MXU_K
SLAB_STATE = N_STATE // N_SLABS
W_CHUNK = 512
VMEM_LIMIT = 58 * 1024 * 1024
ROW_TILE = 512


def _rms_norm(x, g):
    ms = jnp.mean(x * x, axis=-1, keepdims=True)
    return x * lax.rsqrt(ms + RMS_EPS) * g


def _const_spec(shape, index=None):
    index = (0,) * len(shape) if index is None else index
    return pl.BlockSpec(shape, lambda i: index, pipeline_mode=pl.Buffered(1))


def _rows_spec(m, first_step, n_tiles):
    return pl.BlockSpec((m, D_MODEL), lambda i: (jnp.clip(i - first_step, 0, n_tiles - 1), 0))


def _natural_spec(batch, steps, first_step, n_tiles):
    return pl.BlockSpec((batch, steps, D_MODEL),
                        lambda i: (0, jnp.clip(i - first_step, 0, n_tiles - 1), 0))


def _col_chunk_spec(layer, n_rows, n_chunks):
    return pl.BlockSpec((None, n_rows, W_CHUNK), lambda i: (layer, 0, jnp.minimum(i, n_chunks - 1)))


def _row_chunk_spec(layer, n_cols, n_chunks):
    return pl.BlockSpec((None, W_CHUNK, n_cols), lambda i: (layer, jnp.minimum(i, n_chunks - 1), 0))


def _to_time_major(x_ref, slab_ref):
    batch, steps, _ = x_ref.shape
    for c in range(D_MODEL // LANES):
        for b in range(batch):
            slab_ref[c, pl.ds(b, steps, stride=batch), :] = x_ref[b, :, c * LANES:(c + 1) * LANES]
    return jnp.concatenate([slab_ref[c] for c in range(D_MODEL // LANES)], axis=1)


def _from_time_major(y, o_ref, slab_ref):
    batch, steps, _ = o_ref.shape
    for c in range(D_MODEL // LANES):
        slab_ref[c] = y[:, c * LANES:(c + 1) * LANES]
        for b in range(batch):
            o_ref[b, :, c * LANES:(c + 1) * LANES] = slab_ref[c, pl.ds(b, steps, stride=batch), :]


def _s5_rows(x, hre_ref, him_ref, g_ref, are_ref, aim_ref, wb_ref, wc_ref, d_ref, wglu_ref,
             bu_ref, hb_ref, y_ref, *, rows_per_step):
    br = rows_per_step
    m = x.shape[0]
    n_steps = m // br
    steps_per_iter = 2 if (n_steps % 2 == 0 and br < 16) else 1
    lane_width = max(LANES, min(4 * LANES, 4096 // br))
    u = _rms_norm(x, g_ref[...])
    ub = u.astype(jnp.bfloat16)

    for s in range(N_SLABS):
        ch = slice(s * MXU_K, (s + 1) * MXU_K)
        bu_ref[0:m, :] = jnp.dot(ub[:, ch], wb_ref[s], preferred_element_type=jnp.float32)

        for l0 in range(0, SLAB_STATE, lane_width):
            re_l = slice(l0, l0 + lane_width)
            im_l = slice(SLAB_STATE + l0, SLAB_STATE + l0 + lane_width)
            st_l = slice(s * SLAB_STATE + l0, s * SLAB_STATE + l0 + lane_width)
            a_r = are_ref[:, st_l]
            a_i = aim_ref[:, st_l]
            if br <= SUBLANES:
                a_r = jnp.broadcast_to(a_r, (br, lane_width))
                a_i = jnp.broadcast_to(a_i, (br, lane_width))

            def step(it, carry, re_l=re_l, im_l=im_l, a_r=a_r, a_i=a_i):
                h_r, h_i = carry
                r0 = pl.multiple_of(it * (steps_per_iter * br), steps_per_iter * br)
                out_r, out_i = [], []
                for k in range(steps_per_iter):
                    b_r = bu_ref[pl.ds(r0 + k * br, br), re_l]
                    b_i = bu_ref[pl.ds(r0 + k * br, br), im_l]
                    h_r, h_i = a_r * h_r - a_i * h_i + b_r, a_r * h_i + a_i * h_r + b_i
                    out_r.append(h_r)
                    out_i.append(h_i)
                rows = pl.ds(r0, steps_per_iter * br)
                hb_ref[rows, re_l] = jnp.concatenate(out_r, axis=0).astype(jnp.bfloat16)
                hb_ref[rows, im_l] = jnp.concatenate(out_i, axis=0).astype(jnp.bfloat16)
                return h_r, h_i

            h_r, h_i = lax.fori_loop(0, n_steps // steps_per_iter, step,
                                     (hre_ref[:, st_l], him_ref[:, st_l]), unroll=True)
            hre_ref[:, st_l] = h_r
            him_ref[:, st_l] = h_i

        ys = jnp.dot(hb_ref[0:m, :], wc_ref[s], preferred_element_type=jnp.float32)
        ys = ys + d_ref[:, ch] * u[:, ch]
        y_ref[0:m, ch] = jax.nn.gelu(ys).astype(jnp.bfloat16)

    yb = y_ref[0:m, :]
    n_half = D_MODEL // W_CHUNK
    outs = []
    for c in range(n_half):
        z_lin = jnp.dot(yb, wglu_ref[c], preferred_element_type=jnp.float32)
        z_gate = jnp.dot(yb, wglu_ref[n_half + c], preferred_element_type=jnp.float32)
        outs.append(z_lin * jax.nn.sigmoid(z_gate))
    return x + jnp.concatenate(outs, axis=1)


def _s5_kernel(xp_ref, xs_ref, h0re_ref, h0im_ref, g_ref, are_ref, aim_ref, wb_ref, wc_ref, d_ref, wglu_f32_ref,
               op_ref, os_ref, pre_ref, pim_ref, sre_ref, sim_ref,
               wglu_ref, bu_ref, hb_ref, y_ref, *maybe_slab_ref, n_cast, n_tiles, batch):
    i = pl.program_id(0)
    body = functools.partial(_s5_rows, g_ref=g_ref, are_ref=are_ref, aim_ref=aim_ref, wb_ref=wb_ref,
                             wc_ref=wc_ref, d_ref=d_ref, wglu_ref=wglu_ref, bu_ref=bu_ref, hb_ref=hb_ref,
                             y_ref=y_ref)

    @pl.when(i < n_cast)
    def _():
        wglu_ref[i] = wglu_f32_ref[...].astype(jnp.bfloat16)

    @pl.when(i == n_cast)
    def _():
        pre_ref[...] = jnp.zeros_like(pre_ref)
        pim_ref[...] = jnp.zeros_like(pim_ref)

    @pl.when(jnp.logical_and(i >= n_cast, i < n_cast + n_tiles))
    def _():
        x = _to_time_major(xp_ref, maybe_slab_ref[0]) if maybe_slab_ref else xp_ref[...]
        op_ref[...] = body(x, pre_ref, pim_ref, rows_per_step=batch)

    @pl.when(i == n_cast + n_tiles)
    def _():
        sre_ref[...] = h0re_ref[...]
        sim_ref[...] = h0im_ref[...]
        os_ref[...] = body(xs_ref[...], sre_ref, sim_ref, rows_per_step=xs_ref.shape[0])


def _s5_mixer(xp, xs, h0_re, h0_im, gain, s5p, d_skip, w_glu, *, layer, mixer, batch, seq):
    natural = xp.ndim == 3
    a_re, a_im, wb, wc = s5p
    dec = xs.shape[0]
    m = ROW_TILE
    steps = m // batch
    n_tiles = (seq * batch) // m
    n_cast = (2 * D_MODEL) // W_CHUNK
    xp_spec = (_natural_spec(batch, steps, n_cast, n_tiles) if natural else _rows_spec(m, n_cast, n_tiles))
    scratch = [
        pltpu.VMEM((n_cast, D_MODEL, W_CHUNK), jnp.bfloat16),
        pltpu.VMEM((m, 2 * SLAB_STATE), jnp.float32),
        pltpu.VMEM((m, 2 * SLAB_STATE), jnp.bfloat16),
        pltpu.VMEM((m, D_MODEL), jnp.bfloat16),
    ]
    if natural:
        scratch.append(pltpu.VMEM((D_MODEL // LANES, m, LANES), jnp.float32))
    kern = functools.partial(_s5_kernel, n_cast=n_cast, n_tiles=n_tiles, batch=batch)
    return pl.pallas_call(
        kern,
        grid=(n_cast + n_tiles + 1,),
        in_specs=[
            xp_spec,
            _const_spec((dec, D_MODEL)),
            _const_spec((None, dec, N_STATE), (mixer, 0, 0)), _const_spec((None, dec, N_STATE), (mixer, 0, 0)),
            _const_spec((None, 1, D_MODEL), (layer, 0, 0)),
            _const_spec((1, N_STATE)), _const_spec((1, N_STATE)),
            _const_spec((N_SLABS, MXU_K, 2 * SLAB_STATE)),
            _const_spec((N_SLABS, 2 * SLAB_STATE, MXU_K)),
            _const_spec((None, 1, D_MODEL), (mixer, 0, 0)),
            _col_chunk_spec(mixer, D_MODEL, n_cast),
        ],
        out_specs=[
            _rows_spec(m, n_cast, n_tiles),
            pl.BlockSpec((dec, D_MODEL), lambda i: (0, 0)),
            pl.BlockSpec((batch, N_STATE), lambda i: (0, 0)), pl.BlockSpec((batch, N_STATE), lambda i: (0, 0)),
            pl.BlockSpec((dec, N_STATE), lambda i: (0, 0)), pl.BlockSpec((dec, N_STATE), lambda i: (0, 0)),
        ],
        out_shape=[
            jax.ShapeDtypeStruct((seq * batch, D_MODEL), jnp.float32),
            jax.ShapeDtypeStruct((dec, D_MODEL), jnp.float32),
            jax.ShapeDtypeStruct((batch, N_STATE), jnp.float32), jax.ShapeDtypeStruct((batch, N_STATE), jnp.float32),
            jax.ShapeDtypeStruct((dec, N_STATE), jnp.float32), jax.ShapeDtypeStruct((dec, N_STATE), jnp.float32),
        ],
        scratch_shapes=scratch,
        compiler_params=pltpu.CompilerParams(dimension_semantics=("arbitrary",), vmem_limit_bytes=VMEM_LIMIT),
        name="s5_mixer",
    )(xp, xs, h0_re, h0_im, gain, a_re, a_im, wb, wc, d_skip, w_glu)


def _conv_rows(x, g_ref, win_ref, cw_ref, wout_ref, cvx_ref, *, rows_per_step):
    br = rows_per_step
    m = x.shape[0]
    hist = HIST * br
    n_d = D_MODEL // W_CHUNK
    ub = _rms_norm(x, g_ref[...]).astype(jnp.bfloat16)
    bcv = [jnp.dot(ub, win_ref[c], preferred_element_type=jnp.float32) for c in range(3 * n_d)]
    outs = 0.0
    for c in range(n_d):
        cols = slice(c * W_CHUNK, (c + 1) * W_CHUNK)
        cv = bcv[n_d + c] * bcv[2 * n_d + c]
        cvx_ref[hist:hist + m, cols] = cv
        y = cw_ref[CONV_WIDTH - 1:CONV_WIDTH, cols] * cv
        for k in range(CONV_WIDTH - 1):
            y = y + cw_ref[k:k + 1, cols] * cvx_ref[k * br:k * br + m, cols]
        gated = (bcv[c] * y).astype(jnp.bfloat16)
        outs = outs + jnp.dot(gated, wout_ref[c], preferred_element_type=jnp.float32)
    cvx_ref[0:hist, :] = cvx_ref[m:m + hist, :]
    return x + outs


def _conv_kernel(xp_ref, xs_ref, bufs_ref, g_ref, win_f32_ref, cw_ref, wout_f32_ref,
                 op_ref, os_ref, nbufp_ref, nbufs_ref, win_ref, wout_ref, cvx_ref, *, n_cast, n_tiles, batch):
    i = pl.program_id(0)
    dec = xs_ref.shape[0]
    body = functools.partial(_conv_rows, g_ref=g_ref, win_ref=win_ref, cw_ref=cw_ref, wout_ref=wout_ref,
                             cvx_ref=cvx_ref)

    @pl.when(i < n_cast)
    def _():
        win_ref[i] = win_f32_ref[...].astype(jnp.bfloat16)

    @pl.when(i < wout_ref.shape[0])
    def _():
        wout_ref[i] = wout_f32_ref[...].astype(jnp.bfloat16)

    @pl.when(i == n_cast)
    def _():
        cvx_ref[0:HIST * batch, :] = jnp.zeros((HIST * batch, D_MODEL), jnp.float32)

    @pl.when(jnp.logical_and(i >= n_cast, i < n_cast + n_tiles))
    def _():
        op_ref[...] = body(xp_ref[...], rows_per_step=batch)

    @pl.when(i == n_cast + n_tiles - 1)
    def _():
        nbufp_ref[...] = cvx_ref[0:HIST * batch, :]

    @pl.when(i == n_cast + n_tiles)
    def _():
        cvx_ref[0:HIST * dec, :] = bufs_ref[...]
        os_ref[...] = body(xs_ref[...], rows_per_step=dec)
        nbufs_ref[...] = cvx_ref[0:HIST * dec, :]


def _conv_mixer(xp, xs, buf_s, gain, w_in, conv_w, w_out, *, layer, mixer, batch, seq):
    dec = xs.shape[0]
    m = ROW_TILE
    n_tiles = (seq * batch) // m
    n_cast = (3 * D_MODEL) // W_CHUNK
    n_out = D_MODEL // W_CHUNK
    kern = functools.partial(_conv_kernel, n_cast=n_cast, n_tiles=n_tiles, batch=batch)
    return pl.pallas_call(
        kern,
        grid=(n_cast + n_tiles + 1,),
        in_specs=[
            _rows_spec(m, n_cast, n_tiles),
            _const_spec((dec, D_MODEL)),
            _const_spec((None, HIST * dec, D_MODEL), (mixer, 0, 0)),
            _const_spec((None, 1, D_MODEL), (layer, 0, 0)),
            _col_chunk_spec(mixer, D_MODEL, n_cast),
            _const_spec((None, CONV_WIDTH, D_MODEL), (mixer, 0, 0)),
            _row_chunk_spec(mixer, D_MODEL, n_out),
        ],
        out_specs=[
            _rows_spec(m, n_cast, n_tiles),
            pl.BlockSpec((dec, D_MODEL), lambda i: (0, 0)),
            pl.BlockSpec((HIST * batch, D_MODEL), lambda i: (0, 0)),
            pl.BlockSpec((HIST * dec, D_MODEL), lambda i: (0, 0)),
        ],
        out_shape=[
            jax.ShapeDtypeStruct((seq * batch, D_MODEL), jnp.float32),
            jax.ShapeDtypeStruct((dec, D_MODEL), jnp.float32),
            jax.ShapeDtypeStruct((HIST * batch, D_MODEL), jnp.float32),
            jax.ShapeDtypeStruct((HIST * dec, D_MODEL), jnp.float32),
        ],
        scratch_shapes=[
            pltpu.VMEM((n_cast, D_MODEL, W_CHUNK), jnp.bfloat16),
            pltpu.VMEM((n_out, W_CHUNK, D_MODEL), jnp.bfloat16),
            pltpu.VMEM((max(HIST * batch + m, (HIST + 1) * dec), D_MODEL), jnp.float32),
        ],
        compiler_params=pltpu.CompilerParams(dimension_semantics=("arbitrary",), vmem_limit_bytes=VMEM_LIMIT),
        name="conv_mixer",
    )(xp, xs, buf_s, gain, w_in, conv_w, w_out)


def _mlp_rows(x, g_ref, up_ref, down_ref, gf_ref, *, final_norm):
    ub = _rms_norm(x, g_ref[...]).astype(jnp.bfloat16)
    acc = x
    for c in range(D_FF // W_CHUNK):
        h = jnp.dot(ub, up_ref[c], preferred_element_type=jnp.float32)
        h = jnp.maximum(h, 0.0)
        h = (h * h).astype(jnp.bfloat16)
        acc = acc + jnp.dot(h, down_ref[c], preferred_element_type=jnp.float32)
    if final_norm:
        acc = _rms_norm(acc, gf_ref[...])
    return acc


def _mlp_kernel(xp_ref, xs_ref, g_ref, up_f32_ref, down_f32_ref, gf_ref, op_ref, os_ref,
                up_ref, down_ref, *maybe_slab_ref, n_cast, n_tiles, final_norm):
    i = pl.program_id(0)
    body = functools.partial(_mlp_rows, g_ref=g_ref, up_ref=up_ref, down_ref=down_ref, gf_ref=gf_ref,
                             final_norm=final_norm)

    @pl.when(i < n_cast)
    def _():
        up_ref[i] = up_f32_ref[...].astype(jnp.bfloat16)
        down_ref[i] = down_f32_ref[...].astype(jnp.bfloat16)

    @pl.when(jnp.logical_and(i >= n_cast, i < n_cast + n_tiles))
    def _():
        y = body(xp_ref[...])
        if maybe_slab_ref:
            _from_time_major(y, op_ref, maybe_slab_ref[0])
        else:
            op_ref[...] = y

    @pl.when(i == n_cast + n_tiles)
    def _():
        os_ref[...] = body(xs_ref[...])


def _mlp(xp, xs, gain, w_up, w_down, gain_final, *, layer, batch, seq, final_norm):
    dec = xs.shape[0]
    m = ROW_TILE
    steps = m // batch
    n_tiles = (seq * batch) // m
    n_cast = D_FF // W_CHUNK
    scratch = [
        pltpu.VMEM((n_cast, D_MODEL, W_CHUNK), jnp.bfloat16),
        pltpu.VMEM((n_cast, W_CHUNK, D_MODEL), jnp.bfloat16),
    ]
    if final_norm:
        scratch.append(pltpu.VMEM((D_MODEL // LANES, m, LANES), jnp.float32))
        op_spec = _natural_spec(batch, steps, n_cast, n_tiles)
        op_shape = jax.ShapeDtypeStruct((batch, seq, D_MODEL), jnp.float32)
    else:
        op_spec = _rows_spec(m, n_cast, n_tiles)
        op_shape = jax.ShapeDtypeStruct((seq * batch, D_MODEL), jnp.float32)
    kern = functools.partial(_mlp_kernel, n_cast=n_cast, n_tiles=n_tiles, final_norm=final_norm)
    return pl.pallas_call(
        kern,
        grid=(n_cast + n_tiles + 1,),
        in_specs=[
            _rows_spec(m, n_cast, n_tiles),
            _const_spec((dec, D_MODEL)),
            _const_spec((None, 1, D_MODEL), (layer, 0, 0)),
            _col_chunk_spec(layer, D_MODEL, n_cast),
            _row_chunk_spec(layer, D_MODEL, n_cast),
            _const_spec((1, D_MODEL)),
        ],
        out_specs=[op_spec, pl.BlockSpec((dec, D_MODEL), lambda i: (0, 0))],
        out_shape=[op_shape, jax.ShapeDtypeStruct((dec, D_MODEL), jnp.float32)],
        scratch_shapes=scratch,
        compiler_params=pltpu.CompilerParams(dimension_semantics=("arbitrary",), vmem_limit_bytes=VMEM_LIMIT),
        name="mlp",
    )(xp, xs, gain, w_up, w_down, gain_final)


def _s5_params(a_re, a_im, log_dt, b_re, b_im, c_re, c_im):
    dt = jnp.exp(log_dt)[:, None]
    mag = jnp.exp(a_re * dt)
    ab_re = mag * jnp.cos(a_im * dt)
    ab_im = mag * jnp.sin(a_im * dt)
    den = a_re * a_re + a_im * a_im
    q_re = ((ab_re - 1.0) * a_re + ab_im * a_im) / den
    q_im = (ab_im * a_re - (ab_re - 1.0) * a_im) / den
    bb_re = q_re[..., None] * b_re - q_im[..., None] * b_im
    bb_im = q_re[..., None] * b_im + q_im[..., None] * b_re
    gps = MXU_K // GROUP_SIZE
    eye = jnp.eye(gps, dtype=jnp.float32)

    def pack_b(w):
        w = w.reshape(N_SLABS, gps, STATE_DIM, GROUP_SIZE)
        return jnp.einsum('sgph,gk->sghkp', w, eye).reshape(N_SLABS, MXU_K, SLAB_STATE)

    def pack_c(w):
        w = w.reshape(N_SLABS, gps, GROUP_SIZE, STATE_DIM)
        return jnp.einsum('sghp,gk->sgpkh', w, eye).reshape(N_SLABS, SLAB_STATE, MXU_K)

    wb = jnp.concatenate([pack_b(bb_re), pack_b(bb_im)], axis=2).astype(jnp.bfloat16)
    wc = jnp.concatenate([pack_c(c_re), pack_c(-c_im)], axis=1).astype(jnp.bfloat16)
    return ab_re.reshape(1, N_STATE), ab_im.reshape(1, N_STATE), wb, wc


def kernel(x_prompt, x_sample, state_ssm_re, state_ssm_im, state_conv, norm_mix, norm_mlp, norm_final,
           ssm_a_re, ssm_a_im, ssm_log_dt, ssm_b_re, ssm_b_im, ssm_c_re, ssm_c_im, ssm_d, ssm_w_glu,
           conv_w_in, conv_w, conv_w_out, mlp_w_up, mlp_w_down):
    batch, seq, _ = x_prompt.shape
    dec = x_sample.shape[0]
    n_ssm = ssm_a_re.shape[0]
    n_conv = conv_w_in.shape[0]

    g_mix = norm_mix.reshape(DEPTH, 1, D_MODEL)
    g_mlp = norm_mlp.reshape(DEPTH, 1, D_MODEL)
    g_final = norm_final.reshape(1, D_MODEL)
    d_skip = ssm_d.reshape(n_ssm, 1, D_MODEL)
    s5p = [_s5_params(ssm_a_re[j], ssm_a_im[j], ssm_log_dt[j], ssm_b_re[j], ssm_b_im[j],
                      ssm_c_re[j], ssm_c_im[j]) for j in range(n_ssm)]
    h0_re = state_ssm_re.reshape(n_ssm, dec, N_STATE)
    h0_im = state_ssm_im.reshape(n_ssm, dec, N_STATE)
    buf_s = jnp.transpose(state_conv, (0, 2, 1, 3)).reshape(n_conv, HIST * dec, D_MODEL)

    xp = x_prompt
    xs = x_sample.reshape(dec, D_MODEL)
    pre, pim, sre, sim, pbuf, sbuf = [], [], [], [], [], []
    for i in range(DEPTH):
        j = i // 2
        if i % 2 == 0:
            xp, xs, a, b, c, d = _s5_mixer(xp, xs, h0_re, h0_im, g_mix, s5p[j], d_skip, ssm_w_glu,
                                           layer=i, mixer=j, batch=batch, seq=seq)
            pre.append(a), pim.append(b), sre.append(c), sim.append(d)
        else:
            xp, xs, a, b = _conv_mixer(xp, xs, buf_s, g_mix, conv_w_in, conv_w, conv_w_out,
                                       layer=i, mixer=j, batch=batch, seq=seq)
            pbuf.append(a), sbuf.append(b)
        xp, xs = _mlp(xp, xs, g_mlp, mlp_w_up, mlp_w_down, g_final, layer=i, batch=batch, seq=seq,
                      final_norm=(i == DEPTH - 1))

    def states(parts, n):
        return jnp.stack(parts).reshape(n_ssm, n, N_GROUPS, STATE_DIM)

    def bufs(parts, n):
        return jnp.transpose(jnp.stack(parts).reshape(n_conv, HIST, n, D_MODEL), (0, 2, 1, 3))

    return (xp, xs.reshape(dec, 1, D_MODEL), states(pre, batch), states(pim, batch), bufs(pbuf, batch),
            states(sre, dec), states(sim, dec), bufs(sbuf, dec))
```

```python
import functools

import jax
import jax.numpy as jnp
from jax import lax
from jax.experimental import pallas as pl
from jax.experimental.pallas import tpu as pltpu

D_MODEL = 1024
DEPTH = 4
GROUP_SIZE = 16
N_GROUPS = D_MODEL // GROUP_SIZE
STATE_DIM = 64
N_STATE = N_GROUPS * STATE_DIM
CONV_WIDTH = 3
HIST = CONV_WIDTH - 1
D_FF = 4 * D_MODEL
RMS_EPS = 1e-6

LANES = 128
MXU_K = 256
N_SLABS = D_MODEL // MXU_K
SLAB_STATE = N_STATE // N_SLABS
CHUNK = MXU_K // GROUP_SIZE
W_CHUNK = 512
VMEM_LIMIT = 58 * 1024 * 1024
ROW_TILE = 512
S5_TILE = 1024


def _rms_norm(x, g):
    ms = jnp.mean(x * x, axis=-1, keepdims=True)
    return x * lax.rsqrt(ms + RMS_EPS) * g


def _const_spec(shape, index=None):
    index = (0,) * len(shape) if index is None else index
    return pl.BlockSpec(shape, lambda i: index, pipeline_mode=pl.Buffered(1))


def _rows_spec(m, first_step, n_tiles):
    return pl.BlockSpec((m, D_MODEL), lambda i: (jnp.clip(i - first_step, 0, n_tiles - 1), 0))


def _natural_spec(batch, steps, first_step, n_tiles):
    return pl.BlockSpec((batch, steps, D_MODEL),
                        lambda i: (0, jnp.clip(i - first_step, 0, n_tiles - 1), 0))


def _col_chunk_spec(layer, n_rows, n_chunks):
    return pl.BlockSpec((None, n_rows, W_CHUNK), lambda i: (layer, 0, jnp.minimum(i, n_chunks - 1)))


def _row_chunk_spec(layer, n_cols, n_chunks):
    return pl.BlockSpec((None, W_CHUNK, n_cols), lambda i: (layer, jnp.minimum(i, n_chunks - 1), 0))


def _to_time_major(x_ref, slab_ref):
    batch, steps, _ = x_ref.shape
    for c in range(D_MODEL // LANES):
        for b in range(batch):
            slab_ref[c, pl.ds(b, steps, stride=batch), :] = x_ref[b, :, c * LANES:(c + 1) * LANES]
    return jnp.concatenate([slab_ref[c] for c in range(D_MODEL // LANES)], axis=1)


def _from_time_major(y, o_ref, slab_ref):
    batch, steps, _ = o_ref.shape
    for c in range(D_MODEL // LANES):
        slab_ref[c] = y[:, c * LANES:(c + 1) * LANES]
        for b in range(batch):
            o_ref[b, :, c * LANES:(c + 1) * LANES] = slab_ref[c, pl.ds(b, steps, stride=batch), :]


def _glu_tail(x, yb, wglu_ref):
    n_half = D_MODEL // W_CHUNK
    outs = []
    for c in range(n_half):
        z_lin = jnp.dot(yb, wglu_ref[c], preferred_element_type=jnp.float32)
        z_gate = jnp.dot(yb, wglu_ref[n_half + c], preferred_element_type=jnp.float32)
        outs.append(z_lin * jax.nn.sigmoid(z_gate))
    return x + jnp.concatenate(outs, axis=1)


def _s5_step_rows(x, hre_ref, him_ref, g_ref, are_ref, aim_ref, wb_ref, wc_ref, d_ref, wglu_ref,
                  bu_ref, hb_ref, y_ref):
    u = _rms_norm(x, g_ref[...])
    ub = u.astype(jnp.bfloat16)
    for s in range(N_SLABS):
        ch = slice(s * MXU_K, (s + 1) * MXU_K)
        bu_ref[...] = jnp.dot(ub[:, ch], wb_ref[s], preferred_element_type=jnp.float32)
        for l0 in range(0, SLAB_STATE, LANES):
            re_l = slice(l0, l0 + LANES)
            im_l = slice(SLAB_STATE + l0, SLAB_STATE + l0 + LANES)
            st_l = slice(s * SLAB_STATE + l0, s * SLAB_STATE + l0 + LANES)
            a_r, a_i = are_ref[:, st_l], aim_ref[:, st_l]
            h_r, h_i = hre_ref[:, st_l], him_ref[:, st_l]
            n_r = a_r * h_r - a_i * h_i + bu_ref[:, re_l]
            n_i = a_r * h_i + a_i * h_r + bu_ref[:, im_l]
            hre_ref[:, st_l] = n_r
            him_ref[:, st_l] = n_i
            hb_ref[:, re_l] = n_r.astype(jnp.bfloat16)
            hb_ref[:, im_l] = n_i.astype(jnp.bfloat16)
        ys = jnp.dot(hb_ref[...], wc_ref[s], preferred_element_type=jnp.float32)
        ys = ys + d_ref[:, ch] * u[:, ch]
        y_ref[:, ch] = jax.nn.gelu(ys).astype(jnp.bfloat16)
    return _glu_tail(x, y_ref[...], wglu_ref)


def _s5_sample_kernel(xs_ref, h0re_ref, h0im_ref, g_ref, are_ref, aim_ref, wb_ref, wc_ref, d_ref, wglu_f32_ref,
                      os_ref, sre_ref, sim_ref, wglu_ref, bu_ref, hb_ref, y_ref, *, n_cast):
    i = pl.program_id(0)

    @pl.when(i < n_cast)
    def _():
        wglu_ref[i] = wglu_f32_ref[...].astype(jnp.bfloat16)

    @pl.when(i == n_cast)
    def _():
        sre_ref[...] = h0re_ref[...]
        sim_ref[...] = h0im_ref[...]
        os_ref[...] = _s5_step_rows(xs_ref[...], sre_ref, sim_ref, g_ref, are_ref, aim_ref, wb_ref, wc_ref,
                                    d_ref, wglu_ref, bu_ref, hb_ref, y_ref)


def _s5_sample(xs, h0_re, h0_im, gain, step_params, d_skip, w_glu, *, layer, mixer):
    a_re, a_im, wb, wc = step_params
    dec = xs.shape[0]
    n_cast = (2 * D_MODEL) // W_CHUNK
    return pl.pallas_call(
        functools.partial(_s5_sample_kernel, n_cast=n_cast),
        grid=(n_cast + 1,),
        in_specs=[
            _const_spec((dec, D_MODEL)),
            _const_spec((None, dec, N_STATE), (mixer, 0, 0)), _const_spec((None, dec, N_STATE), (mixer, 0, 0)),
            _const_spec((None, 1, D_MODEL), (layer, 0, 0)),
            _const_spec((1, N_STATE)), _const_spec((1, N_STATE)),
            _const_spec((N_SLABS, MXU_K, 2 * SLAB_STATE)),
            _const_spec((N_SLABS, 2 * SLAB_STATE, MXU_K)),
            _const_spec((None, 1, D_MODEL), (mixer, 0, 0)),
            _col_chunk_spec(mixer, D_MODEL, n_cast),
        ],
        out_specs=[
            pl.BlockSpec((dec, D_MODEL), lambda i: (0, 0)),
            pl.BlockSpec((dec, N_STATE), lambda i: (0, 0)), pl.BlockSpec((dec, N_STATE), lambda i: (0, 0)),
        ],
        out_shape=[
            jax.ShapeDtypeStruct((dec, D_MODEL), jnp.float32),
            jax.ShapeDtypeStruct((dec, N_STATE), jnp.float32), jax.ShapeDtypeStruct((dec, N_STATE), jnp.float32),
        ],
        scratch_shapes=[
            pltpu.VMEM((n_cast, D_MODEL, W_CHUNK), jnp.bfloat16),
            pltpu.VMEM((dec, 2 * SLAB_STATE), jnp.float32),
            pltpu.VMEM((dec, 2 * SLAB_STATE), jnp.bfloat16),
            pltpu.VMEM((dec, D_MODEL), jnp.bfloat16),
        ],
        compiler_params=pltpu.CompilerParams(dimension_semantics=("arbitrary",), vmem_limit_bytes=VMEM_LIMIT),
        name="s5_sample",
    )(xs, h0_re, h0_im, gain, a_re, a_im, wb, wc, d_skip, w_glu)


def _block_transpose(sets, lane_block):
    sets = [list(vs) for vs in sets]
    n = len(sets[0])
    d = 1
    while d < n:
        upper = (lane_block & d) != 0
        for vs in sets:
            for i in range(n):
                if i & d == 0:
                    a, b = vs[i], vs[i + d]
                    vs[i] = jnp.where(upper, pltpu.roll(b, GROUP_SIZE * d, 1), a)
                    vs[i + d] = jnp.where(upper, b, pltpu.roll(a, LANES - GROUP_SIZE * d, 1))
        d *= 2
    return sets


def _s5_chunk_rows(x, hst_ref, g_ref, ws_ref, wt_ref, wy_ref, a1_ref, a2_ref, d_ref, y_ref,
                   lhs_ref, sc_ref, hp_ref, yg_ref, *, batch):
    m = x.shape[0]
    n_chunks = m // (CHUNK * batch)
    blocks = LANES // GROUP_SIZE
    n_cols = D_MODEL // LANES
    n_halves = CHUNK // blocks
    pass_chunks = min(n_chunks, 4)
    pass_rows = pass_chunks * batch
    u = _rms_norm(x, g_ref[...])
    lane_block = lax.broadcasted_iota(jnp.int32, (pass_rows, LANES), 1) // GROUP_SIZE

    def step_rows(c, s):
        r0 = (c * CHUNK + s) * batch
        return slice(r0, r0 + batch)

    for c0 in range(0, n_chunks, pass_chunks):
        prow = slice(c0 * batch, c0 * batch + pass_rows)
        sets = [[jnp.concatenate([u[step_rows(c0 + c, k * blocks + s), q * LANES:(q + 1) * LANES]
                                  for c in range(pass_chunks)], axis=0) for s in range(blocks)]
                for q in range(n_cols) for k in range(n_halves)]
        w = _block_transpose(sets, lane_block)
        for q in range(n_cols):
            for j in range(blocks):
                lhs = jnp.concatenate([w[q * n_halves + k][j] for k in range(n_halves)], axis=1)
                lhs_ref[q * blocks + j, prow, :] = lhs.astype(jnp.bfloat16)
    for g in range(N_GROUPS):
        sc_ref[g] = jnp.dot(lhs_ref[g], ws_ref[g], preferred_element_type=jnp.float32)
    for g in range(N_GROUPS):
        a1 = a1_ref[g:g + 1, :]
        a2 = a2_ref[g:g + 1, :]
        sc = sc_ref[g]
        sc_sw = pltpu.roll(sc, STATE_DIM, 1)
        h = hst_ref[g]
        hs = pltpu.roll(h, STATE_DIM, 1)
        h_prev = []
        for c in range(n_chunks):
            h_prev.append(h)
            rows_c = slice(c * batch, (c + 1) * batch)
            h, hs = a1 * h + a2 * hs + sc[rows_c], a1 * hs - a2 * h + sc_sw[rows_c]
        hst_ref[g] = h
        hp_ref[g] = jnp.concatenate(h_prev, axis=0).astype(jnp.bfloat16)
    for g in range(N_GROUPS):
        yg_ref[g] = (jnp.dot(lhs_ref[g], wt_ref[g], preferred_element_type=jnp.float32)
                     + jnp.dot(hp_ref[g], wy_ref[g], preferred_element_type=jnp.float32))
    for c0 in range(0, n_chunks, pass_chunks):
        prow = slice(c0 * batch, c0 * batch + pass_rows)
        sets = [[yg_ref[q * blocks + j, prow, k * LANES:(k + 1) * LANES] for j in range(blocks)]
                for q in range(n_cols) for k in range(n_halves)]
        z = _block_transpose(sets, lane_block)
        for q in range(n_cols):
            for k in range(n_halves):
                for s in range(blocks):
                    for c in range(pass_chunks):
                        y_ref[step_rows(c0 + c, k * blocks + s), q * LANES:(q + 1) * LANES] = (
                            z[q * n_halves + k][s][c * batch:(c + 1) * batch])

    ys = y_ref[...] + d_ref[...] * u
    return jax.nn.gelu(ys).astype(jnp.bfloat16)


def _s5_core_kernel(xp_ref, g_ref, ws_ref, wt_ref, wy_ref, a1_ref, a2_ref, d_ref,
                    yb_ref, pre_ref, pim_ref, hst_ref, y_ref, lhs_ref, sc_ref, hp_ref, yg_ref,
                    *maybe_slab_ref, n_tiles, batch):
    i = pl.program_id(0)

    @pl.when(i == 0)
    def _():
        hst_ref[...] = jnp.zeros_like(hst_ref)

    x = _to_time_major(xp_ref, maybe_slab_ref[0]) if maybe_slab_ref else xp_ref[...]
    yb_ref[...] = _s5_chunk_rows(x, hst_ref, g_ref, ws_ref, wt_ref, wy_ref, a1_ref, a2_ref, d_ref,
                                 y_ref, lhs_ref, sc_ref, hp_ref, yg_ref, batch=batch)

    @pl.when(i == n_tiles - 1)
    def _():
        low = lax.broadcasted_iota(jnp.int32, (batch, LANES), 1) < STATE_DIM
        for g in range(0, N_GROUPS, 2):
            h0, h1 = hst_ref[g], hst_ref[g + 1]
            cols = slice(g * STATE_DIM, (g + 2) * STATE_DIM)
            pre_ref[:, cols] = jnp.where(low, h0, pltpu.roll(h1, STATE_DIM, 1))
            pim_ref[:, cols] = jnp.where(low, pltpu.roll(h0, STATE_DIM, 1), h1)


def _s5_prompt(xp, gain, chunk_params, d_skip, w_glu, *, layer, mixer, batch, seq):
    natural = xp.ndim == 3
    ws, wt, wy, a1, a2 = chunk_params
    m = S5_TILE
    steps = m // batch
    n_tiles = (seq * batch) // m
    n_cast = (2 * D_MODEL) // W_CHUNK

    def xp_spec(first_step):
        return (_natural_spec(batch, steps, first_step, n_tiles) if natural
                else _rows_spec(m, first_step, n_tiles))

    slab = [pltpu.VMEM((D_MODEL // LANES, m, LANES), jnp.float32)] if natural else []
    scratch = [
        pltpu.VMEM((N_GROUPS, batch, 2 * STATE_DIM), jnp.float32),
        pltpu.VMEM((m, D_MODEL), jnp.float32),
        pltpu.VMEM((N_GROUPS, m // CHUNK, MXU_K), jnp.bfloat16),
        pltpu.VMEM((N_GROUPS, m // CHUNK, 2 * STATE_DIM), jnp.float32),
        pltpu.VMEM((N_GROUPS, m // CHUNK, 2 * STATE_DIM), jnp.bfloat16),
        pltpu.VMEM((N_GROUPS, m // CHUNK, MXU_K), jnp.float32),
    ]
    yb, pre, pim = pl.pallas_call(
        functools.partial(_s5_core_kernel, n_tiles=n_tiles, batch=batch),
        grid=(n_tiles,),
        in_specs=[
            xp_spec(0),
            _const_spec((None, 1, D_MODEL), (layer, 0, 0)),
            _const_spec((N_GROUPS, MXU_K, 2 * STATE_DIM)),
            _const_spec((N_GROUPS, MXU_K, MXU_K)),
            _const_spec((N_GROUPS, 2 * STATE_DIM, MXU_K)),
            _const_spec((N_GROUPS, 2 * STATE_DIM)), _const_spec((N_GROUPS, 2 * STATE_DIM)),
            _const_spec((None, 1, D_MODEL), (mixer, 0, 0)),
        ],
        out_specs=[
            _rows_spec(m, 0, n_tiles),
            pl.BlockSpec((batch, N_STATE), lambda i: (0, 0)), pl.BlockSpec((batch, N_STATE), lambda i: (0, 0)),
        ],
        out_shape=[
            jax.ShapeDtypeStruct((seq * batch, D_MODEL), jnp.bfloat16),
            jax.ShapeDtypeStruct((batch, N_STATE), jnp.float32), jax.ShapeDtypeStruct((batch, N_STATE), jnp.float32),
        ],
        scratch_shapes=scratch + slab,
        compiler_params=pltpu.CompilerParams(dimension_semantics=("arbitrary",), vmem_limit_bytes=VMEM_LIMIT),
        name="s5_core",
    )(xp, gain, ws, wt, wy, a1, a2, d_skip)
    out = pl.pallas_call(
        functools.partial(_glu_kernel, n_cast=n_cast),
        grid=(n_cast + n_tiles,),
        in_specs=[xp_spec(n_cast), _rows_spec(m, n_cast, n_tiles), _col_chunk_spec(mixer, D_MODEL, n_cast)],
        out_specs=_rows_spec(m, n_cast, n_tiles),
        out_shape=jax.ShapeDtypeStruct((seq * batch, D_MODEL), jnp.float32),
        scratch_shapes=[pltpu.VMEM((n_cast, D_MODEL, W_CHUNK), jnp.bfloat16)] + slab,
        compiler_params=pltpu.CompilerParams(dimension_semantics=("arbitrary",), vmem_limit_bytes=VMEM_LIMIT),
        name="s5_glu",
    )(xp, yb, w_glu)
    return out, pre, pim


def _glu_kernel(xp_ref, yb_ref, wglu_f32_ref, o_ref, wglu_ref, *maybe_slab_ref, n_cast):
    i = pl.program_id(0)

    @pl.when(i < n_cast)
    def _():
        wglu_ref[i] = wglu_f32_ref[...].astype(jnp.bfloat16)

    @pl.when(i >= n_cast)
    def _():
        x = _to_time_major(xp_ref, maybe_slab_ref[0]) if maybe_slab_ref else xp_ref[...]
        o_ref[...] = _glu_tail(x, yb_ref[...], wglu_ref)


def _conv_rows(x, g_ref, win_ref, cw_ref, wout_ref, cvx_ref, *, rows_per_step):
    br = rows_per_step
    m = x.shape[0]
    hist = HIST * br
    n_d = D_MODEL // W_CHUNK
    ub = _rms_norm(x, g_ref[...]).astype(jnp.bfloat16)
    bcv = [jnp.dot(ub, win_ref[c], preferred_element_type=jnp.float32) for c in range(3 * n_d)]
    outs = 0.0
    for c in range(n_d):
        cols = slice(c * W_CHUNK, (c + 1) * W_CHUNK)
        cv = bcv[n_d + c] * bcv[2 * n_d + c]
        cvx_ref[hist:hist + m, cols] = cv
        y = cw_ref[CONV_WIDTH - 1:CONV_WIDTH, cols] * cv
        for k in range(CONV_WIDTH - 1):
            y = y + cw_ref[k:k + 1, cols] * cvx_ref[k * br:k * br + m, cols]
        gated = (bcv[c] * y).astype(jnp.bfloat16)
        outs = outs + jnp.dot(gated, wout_ref[c], preferred_element_type=jnp.float32)
    cvx_ref[0:hist, :] = cvx_ref[m:m + hist, :]
    return x + outs


def _conv_kernel(xp_ref, xs_ref, bufs_ref, g_ref, win_f32_ref, cw_ref, wout_f32_ref,
                 op_ref, os_ref, nbufp_ref, nbufs_ref, win_ref, wout_ref, cvx_ref, *, n_cast, n_tiles, batch):
    i = pl.program_id(0)
    dec = xs_ref.shape[0]
    body = functools.partial(_conv_rows, g_ref=g_ref, win_ref=win_ref, cw_ref=cw_ref, wout_ref=wout_ref,
                             cvx_ref=cvx_ref)

    @pl.when(i < n_cast)
    def _():
        win_ref[i] = win_f32_ref[...].astype(jnp.bfloat16)

    @pl.when(i < wout_ref.shape[0])
    def _():
        wout_ref[i] = wout_f32_ref[...].astype(jnp.bfloat16)

    @pl.when(i == n_cast)
    def _():
        cvx_ref[0:HIST * batch, :] = jnp.zeros((HIST * batch, D_MODEL), jnp.float32)

    @pl.when(jnp.logical_and(i >= n_cast, i < n_cast + n_tiles))
    def _():
        op_ref[...] = body(xp_ref[...], rows_per_step=batch)

    @pl.when(i == n_cast + n_tiles - 1)
    def _():
        nbufp_ref[...] = cvx_ref[0:HIST * batch, :]

    @pl.when(i == n_cast + n_tiles)
    def _():
        cvx_ref[0:HIST * dec, :] = bufs_ref[...]
        os_ref[...] = body(xs_ref[...], rows_per_step=dec)
        nbufs_ref[...] = cvx_ref[0:HIST * dec, :]


def _conv_mixer(xp, xs, buf_s, gain, w_in, conv_w, w_out, *, layer, mixer, batch, seq):
    dec = xs.shape[0]
    m = ROW_TILE
    n_tiles = (seq * batch) // m
    n_cast = (3 * D_MODEL) // W_CHUNK
    n_out = D_MODEL // W_CHUNK
    kern = functools.partial(_conv_kernel, n_cast=n_cast, n_tiles=n_tiles, batch=batch)
    return pl.pallas_call(
        kern,
        grid=(n_cast + n_tiles + 1,),
        in_specs=[
            _rows_spec(m, n_cast, n_tiles),
            _const_spec((dec, D_MODEL)),
            _const_spec((None, HIST * dec, D_MODEL), (mixer, 0, 0)),
            _const_spec((None, 1, D_MODEL), (layer, 0, 0)),
            _col_chunk_spec(mixer, D_MODEL, n_cast),
            _const_spec((None, CONV_WIDTH, D_MODEL), (mixer, 0, 0)),
            _row_chunk_spec(mixer, D_MODEL, n_out),
        ],
        out_specs=[
            _rows_spec(m, n_cast, n_tiles),
            pl.BlockSpec((dec, D_MODEL), lambda i: (0, 0)),
            pl.BlockSpec((HIST * batch, D_MODEL), lambda i: (0, 0)),
            pl.BlockSpec((HIST * dec, D_MODEL), lambda i: (0, 0)),
        ],
        out_shape=[
            jax.ShapeDtypeStruct((seq * batch, D_MODEL), jnp.float32),
            jax.ShapeDtypeStruct((dec, D_MODEL), jnp.float32),
            jax.ShapeDtypeStruct((HIST * batch, D_MODEL), jnp.float32),
            jax.ShapeDtypeStruct((HIST * dec, D_MODEL), jnp.float32),
        ],
        scratch_shapes=[
            pltpu.VMEM((n_cast, D_MODEL, W_CHUNK), jnp.bfloat16),
            pltpu.VMEM((n_out, W_CHUNK, D_MODEL), jnp.bfloat16),
            pltpu.VMEM((max(HIST * batch + m, (HIST + 1) * dec), D_MODEL), jnp.float32),
        ],
        compiler_params=pltpu.CompilerParams(dimension_semantics=("arbitrary",), vmem_limit_bytes=VMEM_LIMIT),
        name="conv_mixer",
    )(xp, xs, buf_s, gain, w_in, conv_w, w_out)


def _mlp_rows(x, g_ref, up_ref, down_ref, gf_ref, *, final_norm):
    ub = _rms_norm(x, g_ref[...]).astype(jnp.bfloat16)
    acc = x
    for c in range(D_FF // W_CHUNK):
        h = jnp.dot(ub, up_ref[c], preferred_element_type=jnp.float32)
        h = jnp.maximum(h, 0.0)
        h = (h * h).astype(jnp.bfloat16)
        acc = acc + jnp.dot(h, down_ref[c], preferred_element_type=jnp.float32)
    if final_norm:
        acc = _rms_norm(acc, gf_ref[...])
    return acc


def _mlp_kernel(xp_ref, xs_ref, g_ref, up_f32_ref, down_f32_ref, gf_ref, op_ref, os_ref,
                up_ref, down_ref, *maybe_slab_ref, n_cast, n_tiles, final_norm):
    i = pl.program_id(0)
    body = functools.partial(_mlp_rows, g_ref=g_ref, up_ref=up_ref, down_ref=down_ref, gf_ref=gf_ref,
                             final_norm=final_norm)

    @pl.when(i < n_cast)
    def _():
        up_ref[i] = up_f32_ref[...].astype(jnp.bfloat16)
        down_ref[i] = down_f32_ref[...].astype(jnp.bfloat16)

    @pl.when(jnp.logical_and(i >= n_cast, i < n_cast + n_tiles))
    def _():
        y = body(xp_ref[...])
        if maybe_slab_ref:
            _from_time_major(y, op_ref, maybe_slab_ref[0])
        else:
            op_ref[...] = y

    @pl.when(i == n_cast + n_tiles)
    def _():
        os_ref[...] = body(xs_ref[...])


def _mlp(xp, xs, gain, w_up, w_down, gain_final, *, layer, batch, seq, final_norm):
    dec = xs.shape[0]
    m = ROW_TILE
    steps = m // batch
    n_tiles = (seq * batch) // m
    n_cast = D_FF // W_CHUNK
    scratch = [
        pltpu.VMEM((n_cast, D_MODEL, W_CHUNK), jnp.bfloat16),
        pltpu.VMEM((n_cast, W_CHUNK, D_MODEL), jnp.bfloat16),
    ]
    if final_norm:
        scratch.append(pltpu.VMEM((D_MODEL // LANES, m, LANES), jnp.float32))
        op_spec = _natural_spec(batch, steps, n_cast, n_tiles)
        op_shape = jax.ShapeDtypeStruct((batch, seq, D_MODEL), jnp.float32)
    else:
        op_spec = _rows_spec(m, n_cast, n_tiles)
        op_shape = jax.ShapeDtypeStruct((seq * batch, D_MODEL), jnp.float32)
    kern = functools.partial(_mlp_kernel, n_cast=n_cast, n_tiles=n_tiles, final_norm=final_norm)
    return pl.pallas_call(
        kern,
        grid=(n_cast + n_tiles + 1,),
        in_specs=[
            _rows_spec(m, n_cast, n_tiles),
            _const_spec((dec, D_MODEL)),
            _const_spec((None, 1, D_MODEL), (layer, 0, 0)),
            _col_chunk_spec(layer, D_MODEL, n_cast),
            _row_chunk_spec(layer, D_MODEL, n_cast),
            _const_spec((1, D_MODEL)),
        ],
        out_specs=[op_spec, pl.BlockSpec((dec, D_MODEL), lambda i: (0, 0))],
        out_shape=[op_shape, jax.ShapeDtypeStruct((dec, D_MODEL), jnp.float32)],
        scratch_shapes=scratch,
        compiler_params=pltpu.CompilerParams(dimension_semantics=("arbitrary",), vmem_limit_bytes=VMEM_LIMIT),
        name="mlp",
    )(xp, xs, gain, w_up, w_down, gain_final)


def _s5_discretise(a_re, a_im, log_dt, b_re, b_im):
    dt = jnp.exp(log_dt)[:, None]
    k = jnp.arange(CHUNK + 1, dtype=jnp.float32)[:, None, None]
    mag = jnp.exp(k * (a_re * dt))
    pw_re = mag * jnp.cos(k * (a_im * dt))
    pw_im = mag * jnp.sin(k * (a_im * dt))
    ab_re, ab_im = pw_re[1], pw_im[1]
    den = a_re * a_re + a_im * a_im
    q_re = ((ab_re - 1.0) * a_re + ab_im * a_im) / den
    q_im = (ab_im * a_re - (ab_re - 1.0) * a_im) / den
    bb_re = q_re[..., None] * b_re - q_im[..., None] * b_im
    bb_im = q_re[..., None] * b_im + q_im[..., None] * b_re
    return pw_re, pw_im, bb_re, bb_im


def _s5_step_params(pw_re, pw_im, bb_re, bb_im, c_re, c_im):
    gps = MXU_K // GROUP_SIZE
    eye = jnp.eye(gps, dtype=jnp.float32)

    def pack_b(w):
        w = w.reshape(N_SLABS, gps, STATE_DIM, GROUP_SIZE)
        return jnp.einsum('sgph,gk->sghkp', w, eye).reshape(N_SLABS, MXU_K, SLAB_STATE)

    def pack_c(w):
        w = w.reshape(N_SLABS, gps, GROUP_SIZE, STATE_DIM)
        return jnp.einsum('sghp,gk->sgpkh', w, eye).reshape(N_SLABS, SLAB_STATE, MXU_K)

    wb = jnp.concatenate([pack_b(bb_re), pack_b(bb_im)], axis=2).astype(jnp.bfloat16)
    wc = jnp.concatenate([pack_c(c_re), pack_c(-c_im)], axis=1).astype(jnp.bfloat16)
    return pw_re[1].reshape(1, N_STATE), pw_im[1].reshape(1, N_STATE), wb, wc


def _s5_chunk_params(pw_re, pw_im, bb_re, bb_im, c_re, c_im):
    ps_re, ps_im = pw_re[CHUNK - 1::-1], pw_im[CHUNK - 1::-1]
    ws_re = jnp.einsum('sgp,gph->gshp', ps_re, bb_re) - jnp.einsum('sgp,gph->gshp', ps_im, bb_im)
    ws_im = jnp.einsum('sgp,gph->gshp', ps_re, bb_im) + jnp.einsum('sgp,gph->gshp', ps_im, bb_re)
    ws = jnp.concatenate([ws_re, ws_im], axis=-1).reshape(N_GROUPS, MXU_K, 2 * STATE_DIM)
    ca_re = c_re[None] * pw_re[:, :, None, :] - c_im[None] * pw_im[:, :, None, :]
    ca_im = c_re[None] * pw_im[:, :, None, :] + c_im[None] * pw_re[:, :, None, :]
    kern = (jnp.einsum('tgkp,gph->tgkh', ca_re[:CHUNK], bb_re)
            - jnp.einsum('tgkp,gph->tgkh', ca_im[:CHUNK], bb_im))
    lag = jnp.arange(CHUNK)[None, :] - jnp.arange(CHUNK)[:, None]
    sel = jnp.where((lag >= 0)[:, :, None, None, None], kern[jnp.clip(lag, 0, CHUNK - 1)], 0.0)
    wt = jnp.transpose(sel, (2, 0, 4, 1, 3)).reshape(N_GROUPS, MXU_K, MXU_K)
    wy = jnp.concatenate([jnp.transpose(ca_re[1:], (1, 3, 0, 2)), -jnp.transpose(ca_im[1:], (1, 3, 0, 2))],
                         axis=1).reshape(N_GROUPS, 2 * STATE_DIM, MXU_K)
    a1 = jnp.concatenate([pw_re[CHUNK], pw_re[CHUNK]], axis=-1)
    a2 = jnp.concatenate([-pw_im[CHUNK], pw_im[CHUNK]], axis=-1)
    return ws.astype(jnp.bfloat16), wt.astype(jnp.bfloat16), wy.astype(jnp.bfloat16), a1, a2


def kernel(x_prompt, x_sample, state_ssm_re, state_ssm_im, state_conv, norm_mix, norm_mlp, norm_final,
           ssm_a_re, ssm_a_im, ssm_log_dt, ssm_b_re, ssm_b_im, ssm_c_re, ssm_c_im, ssm_d, ssm_w_glu,
           conv_w_in, conv_w, conv_w_out, mlp_w_up, mlp_w_down):
    batch, seq, _ = x_prompt.shape
    dec = x_sample.shape[0]
    n_ssm = ssm_a_re.shape[0]
    n_conv = conv_w_in.shape[0]

    g_mix = norm_mix.reshape(DEPTH, 1, D_MODEL)
    g_mlp = norm_mlp.reshape(DEPTH, 1, D_MODEL)
    g_final = norm_final.reshape(1, D_MODEL)
    d_skip = ssm_d.reshape(n_ssm, 1, D_MODEL)
    step_params, chunk_params = [], []
    for j in range(n_ssm):
        disc = _s5_discretise(ssm_a_re[j], ssm_a_im[j], ssm_log_dt[j], ssm_b_re[j], ssm_b_im[j])
        step_params.append(_s5_step_params(*disc, ssm_c_re[j], ssm_c_im[j]))
        chunk_params.append(_s5_chunk_params(*disc, ssm_c_re[j], ssm_c_im[j]))
    h0_re = state_ssm_re.reshape(n_ssm, dec, N_STATE)
    h0_im = state_ssm_im.reshape(n_ssm, dec, N_STATE)
    buf_s = jnp.transpose(state_conv, (0, 2, 1, 3)).reshape(n_conv, HIST * dec, D_MODEL)

    xp = x_prompt
    xs = x_sample.reshape(dec, D_MODEL)
    pre, pim, sre, sim, pbuf, sbuf = [], [], [], [], [], []
    for i in range(DEPTH):
        j = i // 2
        if i % 2 == 0:
            xp, a, b = _s5_prompt(xp, g_mix, chunk_params[j], d_skip, ssm_w_glu,
                                  layer=i, mixer=j, batch=batch, seq=seq)
            xs, c, d = _s5_sample(xs, h0_re, h0_im, g_mix, step_params[j], d_skip, ssm_w_glu, layer=i, mixer=j)
            pre.append(a), pim.append(b), sre.append(c), sim.append(d)
        else:
            xp, xs, a, b = _conv_mixer(xp, xs, buf_s, g_mix, conv_w_in, conv_w, conv_w_out,
                                       layer=i, mixer=j, batch=batch, seq=seq)
            pbuf.append(a), sbuf.append(b)
        xp, xs = _mlp(xp, xs, g_mlp, mlp_w_up, mlp_w_down, g_final, layer=i, batch=batch, seq=seq,
                      final_norm=(i == DEPTH - 1))

    def states(parts, n):
        return jnp.stack(parts).reshape(n_ssm, n, N_GROUPS, STATE_DIM)

    def bufs(parts, n):
        return jnp.transpose(jnp.stack(parts).reshape(n_conv, HIST, n, D_MODEL), (0, 2, 1, 3))

    return (xp, xs.reshape(dec, 1, D_MODEL), states(pre, batch), states(pim, batch), bufs(pbuf, batch),
            states(sre, dec), states(sim, dec), bufs(sbuf, dec))
```

```python
import functools

import jax
import jax.numpy as jnp
from jax import lax
from jax.experimental import pallas as pl
from jax.experimental.pallas import tpu as pltpu

D_MODEL = 1024
DEPTH = 4
GROUP_SIZE = 16
N_GROUPS = D_MODEL // GROUP_SIZE
STATE_DIM = 64
N_STATE = N_GROUPS * STATE_DIM
CONV_WIDTH = 3
HIST = CONV_WIDTH - 1
D_FF = 4 * D_MODEL
RMS_EPS = 1e-6

LANES = 128
MXU_K = 256
N_SLABS = D_MODEL // MXU_K
SLAB_STATE = N_STATE // N_SLABS
CHUNK = MXU_K // GROUP_SIZE
W_CHUNK = 512
VMEM_LIMIT = 58 * 1024 * 1024
ROW_TILE = 512
S5_TILE = 1024


def _rms_norm(x, g):
    ms = jnp.mean(x * x, axis=-1, keepdims=True)
    return x * lax.rsqrt(ms + RMS_EPS) * g


def _const_spec(shape, index=None):
    index = (0,) * len(shape) if index is None else index
    return pl.BlockSpec(shape, lambda i: index, pipeline_mode=pl.Buffered(1))


def _rows_spec(m, first_step, n_tiles):
    return pl.BlockSpec((m, D_MODEL), lambda i: (jnp.clip(i - first_step, 0, n_tiles - 1), 0))


def _natural_spec(batch, steps, first_step, n_tiles):
    return pl.BlockSpec((batch, steps, D_MODEL),
                        lambda i: (0, jnp.clip(i - first_step, 0, n_tiles - 1), 0))


def _col_chunk_spec(layer, n_rows, n_chunks):
    return pl.BlockSpec((None, n_rows, W_CHUNK), lambda i: (layer, 0, jnp.minimum(i, n_chunks - 1)))


def _row_chunk_spec(layer, n_cols, n_chunks):
    return pl.BlockSpec((None, W_CHUNK, n_cols), lambda i: (layer, jnp.minimum(i, n_chunks - 1), 0))


def _to_time_major(x_ref, slab_ref):
    batch, steps, _ = x_ref.shape
    for c in range(D_MODEL // LANES):
        for b in range(batch):
            slab_ref[c, pl.ds(b, steps, stride=batch), :] = x_ref[b, :, c * LANES:(c + 1) * LANES]
    return jnp.concatenate([slab_ref[c] for c in range(D_MODEL // LANES)], axis=1)


def _from_time_major(y, o_ref, slab_ref):
    batch, steps, _ = o_ref.shape
    for c in range(D_MODEL // LANES):
        slab_ref[c] = y[:, c * LANES:(c + 1) * LANES]
        for b in range(batch):
            o_ref[b, :, c * LANES:(c + 1) * LANES] = slab_ref[c, pl.ds(b, steps, stride=batch), :]


def _glu_tail(x, yb, wglu_ref):
    n_half = D_MODEL // W_CHUNK
    outs = []
    for c in range(n_half):
        z_lin = jnp.dot(yb, wglu_ref[c], preferred_element_type=jnp.float32)
        z_gate = jnp.dot(yb, wglu_ref[n_half + c], preferred_element_type=jnp.float32)
        outs.append(z_lin * jax.nn.sigmoid(z_gate))
    return x + jnp.concatenate(outs, axis=1)


def _s5_step_rows(x, hre_ref, him_ref, g_ref, are_ref, aim_ref, wb_ref, wc_ref, d_ref, wglu_ref,
                  bu_ref, hb_ref, y_ref):
    u = _rms_norm(x, g_ref[...])
    ub = u.astype(jnp.bfloat16)
    for s in range(N_SLABS):
        ch = slice(s * MXU_K, (s + 1) * MXU_K)
        bu_ref[...] = jnp.dot(ub[:, ch], wb_ref[s], preferred_element_type=jnp.float32)
        for l0 in range(0, SLAB_STATE, LANES):
            re_l = slice(l0, l0 + LANES)
            im_l = slice(SLAB_STATE + l0, SLAB_STATE + l0 + LANES)
            st_l = slice(s * SLAB_STATE + l0, s * SLAB_STATE + l0 + LANES)
            a_r, a_i = are_ref[:, st_l], aim_ref[:, st_l]
            h_r, h_i = hre_ref[:, st_l], him_ref[:, st_l]
            n_r = a_r * h_r - a_i * h_i + bu_ref[:, re_l]
            n_i = a_r * h_i + a_i * h_r + bu_ref[:, im_l]
            hre_ref[:, st_l] = n_r
            him_ref[:, st_l] = n_i
            hb_ref[:, re_l] = n_r.astype(jnp.bfloat16)
            hb_ref[:, im_l] = n_i.astype(jnp.bfloat16)
        ys = jnp.dot(hb_ref[...], wc_ref[s], preferred_element_type=jnp.float32)
        ys = ys + d_ref[:, ch] * u[:, ch]
        y_ref[:, ch] = jax.nn.gelu(ys).astype(jnp.bfloat16)
    return _glu_tail(x, y_ref[...], wglu_ref)


def _s5_sample_kernel(xs_ref, h0re_ref, h0im_ref, g_ref, are_ref, aim_ref, wb_ref, wc_ref, d_ref, wglu_f32_ref,
                      os_ref, sre_ref, sim_ref, wglu_ref, bu_ref, hb_ref, y_ref, *, n_cast):
    i = pl.program_id(0)

    @pl.when(i < n_cast)
    def _():
        wglu_ref[i] = wglu_f32_ref[...].astype(jnp.bfloat16)

    @pl.when(i == n_cast)
    def _():
        sre_ref[...] = h0re_ref[...]
        sim_ref[...] = h0im_ref[...]
        os_ref[...] = _s5_step_rows(xs_ref[...], sre_ref, sim_ref, g_ref, are_ref, aim_ref, wb_ref, wc_ref,
                                    d_ref, wglu_ref, bu_ref, hb_ref, y_ref)


def _s5_sample(xs, h0_re, h0_im, gain, step_params, d_skip, w_glu, *, layer, mixer):
    a_re, a_im, wb, wc = step_params
    dec = xs.shape[0]
    n_cast = (2 * D_MODEL) // W_CHUNK
    return pl.pallas_call(
        functools.partial(_s5_sample_kernel, n_cast=n_cast),
        grid=(n_cast + 1,),
        in_specs=[
            _const_spec((dec, D_MODEL)),
            _const_spec((None, dec, N_STATE), (mixer, 0, 0)), _const_spec((None, dec, N_STATE), (mixer, 0, 0)),
            _const_spec((None, 1, D_MODEL), (layer, 0, 0)),
            _const_spec((1, N_STATE)), _const_spec((1, N_STATE)),
            _const_spec((N_SLABS, MXU_K, 2 * SLAB_STATE)),
            _const_spec((N_SLABS, 2 * SLAB_STATE, MXU_K)),
            _const_spec((None, 1, D_MODEL), (mixer, 0, 0)),
            _col_chunk_spec(mixer, D_MODEL, n_cast),
        ],
        out_specs=[
            pl.BlockSpec((dec, D_MODEL), lambda i: (0, 0)),
            pl.BlockSpec((dec, N_STATE), lambda i: (0, 0)), pl.BlockSpec((dec, N_STATE), lambda i: (0, 0)),
        ],
        out_shape=[
            jax.ShapeDtypeStruct((dec, D_MODEL), jnp.float32),
            jax.ShapeDtypeStruct((dec, N_STATE), jnp.float32), jax.ShapeDtypeStruct((dec, N_STATE), jnp.float32),
        ],
        scratch_shapes=[
            pltpu.VMEM((n_cast, D_MODEL, W_CHUNK), jnp.bfloat16),
            pltpu.VMEM((dec, 2 * SLAB_STATE), jnp.float32),
            pltpu.VMEM((dec, 2 * SLAB_STATE), jnp.bfloat16),
            pltpu.VMEM((dec, D_MODEL), jnp.bfloat16),
        ],
        compiler_params=pltpu.CompilerParams(dimension_semantics=("arbitrary",), vmem_limit_bytes=VMEM_LIMIT),
        name="s5_sample",
    )(xs, h0_re, h0_im, gain, a_re, a_im, wb, wc, d_skip, w_glu)


def _block_transpose(sets, lane_block):
    sets = [list(vs) for vs in sets]
    n = len(sets[0])
    d = 1
    while d < n:
        upper = (lane_block & d) != 0
        for vs in sets:
            for i in range(n):
                if i & d == 0:
                    a, b = vs[i], vs[i + d]
                    vs[i] = jnp.where(upper, pltpu.roll(b, GROUP_SIZE * d, 1), a)
                    vs[i + d] = jnp.where(upper, b, pltpu.roll(a, LANES - GROUP_SIZE * d, 1))
        d *= 2
    return sets


def _s5_chunk_rows(x, hst_ref, g_ref, ws_ref, wt_ref, wy_ref, a1_ref, a2_ref, d_ref, y_ref,
                   lhs_ref, sc_ref, hp_ref, yg_ref, *, batch):
    m = x.shape[0]
    n_chunks = m // (CHUNK * batch)
    blocks = LANES // GROUP_SIZE
    n_cols = D_MODEL // LANES
    n_halves = CHUNK // blocks
    pass_chunks = min(n_chunks, 4)
    pass_rows = pass_chunks * batch
    u = _rms_norm(x, g_ref[...])
    lane_block = lax.broadcasted_iota(jnp.int32, (pass_rows, LANES), 1) // GROUP_SIZE

    def step_rows(c, s):
        r0 = (c * CHUNK + s) * batch
        return slice(r0, r0 + batch)

    for c0 in range(0, n_chunks, pass_chunks):
        prow = slice(c0 * batch, c0 * batch + pass_rows)
        sets = [[jnp.concatenate([u[step_rows(c0 + c, k * blocks + s), q * LANES:(q + 1) * LANES]
                                  for c in range(pass_chunks)], axis=0) for s in range(blocks)]
                for q in range(n_cols) for k in range(n_halves)]
        w = _block_transpose(sets, lane_block)
        for q in range(n_cols):
            for j in range(blocks):
                lhs = jnp.concatenate([w[q * n_halves + k][j] for k in range(n_halves)], axis=1)
                lhs_ref[q * blocks + j, prow, :] = lhs.astype(jnp.bfloat16)
    for g in range(N_GROUPS):
        sc_ref[g] = jnp.dot(lhs_ref[g], ws_ref[g], preferred_element_type=jnp.float32)
    for g in range(N_GROUPS):
        a1 = a1_ref[g:g + 1, :]
        a2 = a2_ref[g:g + 1, :]
        sc = sc_ref[g]
        sc_sw = pltpu.roll(sc, STATE_DIM, 1)
        h = hst_ref[g]
        hs = pltpu.roll(h, STATE_DIM, 1)
        h_prev = []
        for c in range(n_chunks):
            h_prev.append(h)
            rows_c = slice(c * batch, (c + 1) * batch)
            h, hs = a1 * h + a2 * hs + sc[rows_c], a1 * hs - a2 * h + sc_sw[rows_c]
        hst_ref[g] = h
        hp_ref[g] = jnp.concatenate(h_prev, axis=0).astype(jnp.bfloat16)
    for g in range(N_GROUPS):
        yg_ref[g] = (jnp.dot(lhs_ref[g], wt_ref[g], preferred_element_type=jnp.float32)
                     + jnp.dot(hp_ref[g], wy_ref[g], preferred_element_type=jnp.float32))
    for c0 in range(0, n_chunks, pass_chunks):
        prow = slice(c0 * batch, c0 * batch + pass_rows)
        sets = [[yg_ref[q * blocks + j, prow, k * LANES:(k + 1) * LANES] for j in range(blocks)]
                for q in range(n_cols) for k in range(n_halves)]
        z = _block_transpose(sets, lane_block)
        for q in range(n_cols):
            for k in range(n_halves):
                for s in range(blocks):
                    for c in range(pass_chunks):
                        y_ref[step_rows(c0 + c, k * blocks + s), q * LANES:(q + 1) * LANES] = (
                            z[q * n_halves + k][s][c * batch:(c + 1) * batch])

    ys = y_ref[...] + d_ref[...] * u
    return jax.nn.gelu(ys).astype(jnp.bfloat16)


def _s5_core_kernel(xp_ref, g_ref, ws_ref, kc_ref, wy_ref, a1_ref, a2_ref, d_ref,
                    yb_ref, pre_ref, pim_ref, wt_ref, hst_ref, y_ref, lhs_ref, sc_ref, hp_ref, yg_ref,
                    *maybe_slab_ref, n_tiles, batch):
    i = pl.program_id(0)

    @pl.when(i == 0)
    def _():
        hst_ref[...] = jnp.zeros_like(hst_ref)
        lane = lax.broadcasted_iota(jnp.int32, (GROUP_SIZE, MXU_K), 1)

        def build(g, carry):
            kc = kc_ref[g]
            for s in range(CHUNK):
                blk = kc if s == 0 else jnp.where(lane >= s * GROUP_SIZE, pltpu.roll(kc, s * GROUP_SIZE, 1), 0.0)
                wt_ref[g, s * GROUP_SIZE:(s + 1) * GROUP_SIZE, :] = blk.astype(jnp.bfloat16)
            return carry

        lax.fori_loop(0, N_GROUPS, build, 0)

    x = _to_time_major(xp_ref, maybe_slab_ref[0]) if maybe_slab_ref else xp_ref[...]
    yb_ref[...] = _s5_chunk_rows(x, hst_ref, g_ref, ws_ref, wt_ref, wy_ref, a1_ref, a2_ref, d_ref,
                                 y_ref, lhs_ref, sc_ref, hp_ref, yg_ref, batch=batch)

    @pl.when(i == n_tiles - 1)
    def _():
        low = lax.broadcasted_iota(jnp.int32, (batch, LANES), 1) < STATE_DIM
        for g in range(0, N_GROUPS, 2):
            h0, h1 = hst_ref[g], hst_ref[g + 1]
            cols = slice(g * STATE_DIM, (g + 2) * STATE_DIM)
            pre_ref[:, cols] = jnp.where(low, h0, pltpu.roll(h1, STATE_DIM, 1))
            pim_ref[:, cols] = jnp.where(low, pltpu.roll(h0, STATE_DIM, 1), h1)


def _s5_prompt(xp, gain, chunk_params, d_skip, w_glu, *, layer, mixer, batch, seq):
    natural = xp.ndim == 3
    ws, kc, wy, a1, a2 = chunk_params
    m = S5_TILE
    steps = m // batch
    n_tiles = (seq * batch) // m
    n_cast = (2 * D_MODEL) // W_CHUNK

    def xp_spec(first_step):
        return (_natural_spec(batch, steps, first_step, n_tiles) if natural
                else _rows_spec(m, first_step, n_tiles))

    slab = [pltpu.VMEM((D_MODEL // LANES, m, LANES), jnp.float32)] if natural else []
    scratch = [
        pltpu.VMEM((N_GROUPS, MXU_K, MXU_K), jnp.bfloat16),
        pltpu.VMEM((N_GROUPS, batch, 2 * STATE_DIM), jnp.float32),
        pltpu.VMEM((m, D_MODEL), jnp.float32),
        pltpu.VMEM((N_GROUPS, m // CHUNK, MXU_K), jnp.bfloat16),
        pltpu.VMEM((N_GROUPS, m // CHUNK, 2 * STATE_DIM), jnp.float32),
        pltpu.VMEM((N_GROUPS, m // CHUNK, 2 * STATE_DIM), jnp.bfloat16),
        pltpu.VMEM((N_GROUPS, m // CHUNK, MXU_K), jnp.float32),
    ]
    yb, pre, pim = pl.pallas_call(
        functools.partial(_s5_core_kernel, n_tiles=n_tiles, batch=batch),
        grid=(n_tiles,),
        in_specs=[
            xp_spec(0),
            _const_spec((None, 1, D_MODEL), (layer, 0, 0)),
            _const_spec((N_GROUPS, MXU_K, 2 * STATE_DIM)),
            _const_spec((N_GROUPS, GROUP_SIZE, MXU_K)),
            _const_spec((N_GROUPS, 2 * STATE_DIM, MXU_K)),
            _const_spec((N_GROUPS, 2 * STATE_DIM)), _const_spec((N_GROUPS, 2 * STATE_DIM)),
            _const_spec((None, 1, D_MODEL), (mixer, 0, 0)),
        ],
        out_specs=[
            _rows_spec(m, 0, n_tiles),
            pl.BlockSpec((batch, N_STATE), lambda i: (0, 0)), pl.BlockSpec((batch, N_STATE), lambda i: (0, 0)),
        ],
        out_shape=[
            jax.ShapeDtypeStruct((seq * batch, D_MODEL), jnp.bfloat16),
            jax.ShapeDtypeStruct((batch, N_STATE), jnp.float32), jax.ShapeDtypeStruct((batch, N_STATE), jnp.float32),
        ],
        scratch_shapes=scratch + slab,
        compiler_params=pltpu.CompilerParams(dimension_semantics=("arbitrary",), vmem_limit_bytes=VMEM_LIMIT),
        name="s5_core",
    )(xp, gain, ws, kc, wy, a1, a2, d_skip)
    out = pl.pallas_call(
        functools.partial(_glu_kernel, n_cast=n_cast),
        grid=(n_cast + n_tiles,),
        in_specs=[xp_spec(n_cast), _rows_spec(m, n_cast, n_tiles), _col_chunk_spec(mixer, D_MODEL, n_cast)],
        out_specs=_rows_spec(m, n_cast, n_tiles),
        out_shape=jax.ShapeDtypeStruct((seq * batch, D_MODEL), jnp.float32),
        scratch_shapes=[pltpu.VMEM((n_cast, D_MODEL, W_CHUNK), jnp.bfloat16)] + slab,
        compiler_params=pltpu.CompilerParams(dimension_semantics=("arbitrary",), vmem_limit_bytes=VMEM_LIMIT),
        name="s5_glu",
    )(xp, yb, w_glu)
    return out, pre, pim


def _glu_kernel(xp_ref, yb_ref, wglu_f32_ref, o_ref, wglu_ref, *maybe_slab_ref, n_cast):
    i = pl.program_id(0)

    @pl.when(i < n_cast)
    def _():
        wglu_ref[i] = wglu_f32_ref[...].astype(jnp.bfloat16)

    @pl.when(i >= n_cast)
    def _():
        x = _to_time_major(xp_ref, maybe_slab_ref[0]) if maybe_slab_ref else xp_ref[...]
        o_ref[...] = _glu_tail(x, yb_ref[...], wglu_ref)


def _conv_rows(x, g_ref, win_ref, cw_ref, wout_ref, cvx_ref, *, rows_per_step):
    br = rows_per_step
    m = x.shape[0]
    hist = HIST * br
    n_d = D_MODEL // W_CHUNK
    ub = _rms_norm(x, g_ref[...]).astype(jnp.bfloat16)
    bcv = [jnp.dot(ub, win_ref[c], preferred_element_type=jnp.float32) for c in range(3 * n_d)]
    outs = 0.0
    for c in range(n_d):
        cols = slice(c * W_CHUNK, (c + 1) * W_CHUNK)
        cv = bcv[n_d + c] * bcv[2 * n_d + c]
        cvx_ref[hist:hist + m, cols] = cv
        y = cw_ref[CONV_WIDTH - 1:CONV_WIDTH, cols] * cv
        for k in range(CONV_WIDTH - 1):
            y = y + cw_ref[k:k + 1, cols] * cvx_ref[k * br:k * br + m, cols]
        gated = (bcv[c] * y).astype(jnp.bfloat16)
        outs = outs + jnp.dot(gated, wout_ref[c], preferred_element_type=jnp.float32)
    cvx_ref[0:hist, :] = cvx_ref[m:m + hist, :]
    return x + outs


def _conv_kernel(xp_ref, xs_ref, bufs_ref, g_ref, win_f32_ref, cw_ref, wout_f32_ref,
                 op_ref, os_ref, nbufp_ref, nbufs_ref, win_ref, wout_ref, cvx_ref, *, n_cast, n_tiles, batch):
    i = pl.program_id(0)
    dec = xs_ref.shape[0]
    body = functools.partial(_conv_rows, g_ref=g_ref, win_ref=win_ref, cw_ref=cw_ref, wout_ref=wout_ref,
                             cvx_ref=cvx_ref)

    @pl.when(i < n_cast)
    def _():
        win_ref[i] = win_f32_ref[...].astype(jnp.bfloat16)

    @pl.when(i < wout_ref.shape[0])
    def _():
        wout_ref[i] = wout_f32_ref[...].astype(jnp.bfloat16)

    @pl.when(i == n_cast)
    def _():
        cvx_ref[0:HIST * batch, :] = jnp.zeros((HIST * batch, D_MODEL), jnp.float32)

    @pl.when(jnp.logical_and(i >= n_cast, i < n_cast + n_tiles))
    def _():
        op_ref[...] = body(xp_ref[...], rows_per_step=batch)

    @pl.when(i == n_cast + n_tiles - 1)
    def _():
        nbufp_ref[...] = cvx_ref[0:HIST * batch, :]

    @pl.when(i == n_cast + n_tiles)
    def _():
        cvx_ref[0:HIST * dec, :] = bufs_ref[...]
        os_ref[...] = body(xs_ref[...], rows_per_step=dec)
        nbufs_ref[...] = cvx_ref[0:HIST * dec, :]


def _conv_mixer(xp, xs, buf_s, gain, w_in, conv_w, w_out, *, layer, mixer, batch, seq):
    dec = xs.shape[0]
    m = ROW_TILE
    n_tiles = (seq * batch) // m
    n_cast = (3 * D_MODEL) // W_CHUNK
    n_out = D_MODEL // W_CHUNK
    kern = functools.partial(_conv_kernel, n_cast=n_cast, n_tiles=n_tiles, batch=batch)
    return pl.pallas_call(
        kern,
        grid=(n_cast + n_tiles + 1,),
        in_specs=[
            _rows_spec(m, n_cast, n_tiles),
            _const_spec((dec, D_MODEL)),
            _const_spec((None, HIST * dec, D_MODEL), (mixer, 0, 0)),
            _const_spec((None, 1, D_MODEL), (layer, 0, 0)),
            _col_chunk_spec(mixer, D_MODEL, n_cast),
            _const_spec((None, CONV_WIDTH, D_MODEL), (mixer, 0, 0)),
            _row_chunk_spec(mixer, D_MODEL, n_out),
        ],
        out_specs=[
            _rows_spec(m, n_cast, n_tiles),
            pl.BlockSpec((dec, D_MODEL), lambda i: (0, 0)),
            pl.BlockSpec((HIST * batch, D_MODEL), lambda i: (0, 0)),
            pl.BlockSpec((HIST * dec, D_MODEL), lambda i: (0, 0)),
        ],
        out_shape=[
            jax.ShapeDtypeStruct((seq * batch, D_MODEL), jnp.float32),
            jax.ShapeDtypeStruct((dec, D_MODEL), jnp.float32),
            jax.ShapeDtypeStruct((HIST * batch, D_MODEL), jnp.float32),
            jax.ShapeDtypeStruct((HIST * dec, D_MODEL), jnp.float32),
        ],
        scratch_shapes=[
            pltpu.VMEM((n_cast, D_MODEL, W_CHUNK), jnp.bfloat16),
            pltpu.VMEM((n_out, W_CHUNK, D_MODEL), jnp.bfloat16),
            pltpu.VMEM((max(HIST * batch + m, (HIST + 1) * dec), D_MODEL), jnp.float32),
        ],
        compiler_params=pltpu.CompilerParams(dimension_semantics=("arbitrary",), vmem_limit_bytes=VMEM_LIMIT),
        name="conv_mixer",
    )(xp, xs, buf_s, gain, w_in, conv_w, w_out)


def _mlp_rows(x, g_ref, up_ref, down_ref, gf_ref, *, final_norm):
    ub = _rms_norm(x, g_ref[...]).astype(jnp.bfloat16)
    acc = x
    for c in range(D_FF // W_CHUNK):
        h = jnp.dot(ub, up_ref[c], preferred_element_type=jnp.float32)
        h = jnp.maximum(h, 0.0)
        h = (h * h).astype(jnp.bfloat16)
        acc = acc + jnp.dot(h, down_ref[c], preferred_element_type=jnp.float32)
    if final_norm:
        acc = _rms_norm(acc, gf_ref[...])
    return acc


def _mlp_kernel(xp_ref, xs_ref, g_ref, up_f32_ref, down_f32_ref, gf_ref, op_ref, os_ref,
                up_ref, down_ref, *maybe_slab_ref, n_cast, n_tiles, final_norm):
    i = pl.program_id(0)
    body = functools.partial(_mlp_rows, g_ref=g_ref, up_ref=up_ref, down_ref=down_ref, gf_ref=gf_ref,
                             final_norm=final_norm)

    @pl.when(i < n_cast)
    def _():
        up_ref[i] = up_f32_ref[...].astype(jnp.bfloat16)
        down_ref[i] = down_f32_ref[...].astype(jnp.bfloat16)

    @pl.when(jnp.logical_and(i >= n_cast, i < n_cast + n_tiles))
    def _():
        y = body(xp_ref[...])
        if maybe_slab_ref:
            _from_time_major(y, op_ref, maybe_slab_ref[0])
        else:
            op_ref[...] = y

    @pl.when(i == n_cast + n_tiles)
    def _():
        os_ref[...] = body(xs_ref[...])


def _mlp(xp, xs, gain, w_up, w_down, gain_final, *, layer, batch, seq, final_norm):
    dec = xs.shape[0]
    m = ROW_TILE
    steps = m // batch
    n_tiles = (seq * batch) // m
    n_cast = D_FF // W_CHUNK
    scratch = [
        pltpu.VMEM((n_cast, D_MODEL, W_CHUNK), jnp.bfloat16),
        pltpu.VMEM((n_cast, W_CHUNK, D_MODEL), jnp.bfloat16),
    ]
    if final_norm:
        scratch.append(pltpu.VMEM((D_MODEL // LANES, m, LANES), jnp.float32))
        op_spec = _natural_spec(batch, steps, n_cast, n_tiles)
        op_shape = jax.ShapeDtypeStruct((batch, seq, D_MODEL), jnp.float32)
    else:
        op_spec = _rows_spec(m, n_cast, n_tiles)
        op_shape = jax.ShapeDtypeStruct((seq * batch, D_MODEL), jnp.float32)
    kern = functools.partial(_mlp_kernel, n_cast=n_cast, n_tiles=n_tiles, final_norm=final_norm)
    return pl.pallas_call(
        kern,
        grid=(n_cast + n_tiles + 1,),
        in_specs=[
            _rows_spec(m, n_cast, n_tiles),
            _const_spec((dec, D_MODEL)),
            _const_spec((None, 1, D_MODEL), (layer, 0, 0)),
            _col_chunk_spec(layer, D_MODEL, n_cast),
            _row_chunk_spec(layer, D_MODEL, n_cast),
            _const_spec((1, D_MODEL)),
        ],
        out_specs=[op_spec, pl.BlockSpec((dec, D_MODEL), lambda i: (0, 0))],
        out_shape=[op_shape, jax.ShapeDtypeStruct((dec, D_MODEL), jnp.float32)],
        scratch_shapes=scratch,
        compiler_params=pltpu.CompilerParams(dimension_semantics=("arbitrary",), vmem_limit_bytes=VMEM_LIMIT),
        name="mlp",
    )(xp, xs, gain, w_up, w_down, gain_final)


def _s5_discretise(a_re, a_im, log_dt, b_re, b_im):
    dt = jnp.exp(log_dt)[:, None]
    k = jnp.arange(CHUNK + 1, dtype=jnp.float32)[:, None, None]
    mag = jnp.exp(k * (a_re * dt))
    pw_re = mag * jnp.cos(k * (a_im * dt))
    pw_im = mag * jnp.sin(k * (a_im * dt))
    ab_re, ab_im = pw_re[1], pw_im[1]
    den = a_re * a_re + a_im * a_im
    q_re = ((ab_re - 1.0) * a_re + ab_im * a_im) / den
    q_im = (ab_im * a_re - (ab_re - 1.0) * a_im) / den
    bb_re = q_re[..., None] * b_re - q_im[..., None] * b_im
    bb_im = q_re[..., None] * b_im + q_im[..., None] * b_re
    return pw_re, pw_im, bb_re, bb_im


def _s5_step_params(pw_re, pw_im, bb_re, bb_im, c_re, c_im):
    gps = MXU_K // GROUP_SIZE
    eye = jnp.eye(gps, dtype=jnp.float32)

    def pack_b(w):
        w = w.reshape(N_SLABS, gps, STATE_DIM, GROUP_SIZE)
        return jnp.einsum('sgph,gk->sghkp', w, eye).reshape(N_SLABS, MXU_K, SLAB_STATE)

    def pack_c(w):
        w = w.reshape(N_SLABS, gps, GROUP_SIZE, STATE_DIM)
        return jnp.einsum('sghp,gk->sgpkh', w, eye).reshape(N_SLABS, SLAB_STATE, MXU_K)

    wb = jnp.concatenate([pack_b(bb_re), pack_b(bb_im)], axis=2).astype(jnp.bfloat16)
    wc = jnp.concatenate([pack_c(c_re), pack_c(-c_im)], axis=1).astype(jnp.bfloat16)
    return pw_re[1].reshape(1, N_STATE), pw_im[1].reshape(1, N_STATE), wb, wc


def _s5_chunk_params(pw_re, pw_im, bb_re, bb_im, c_re, c_im):
    ps_re = jnp.transpose(pw_re[CHUNK - 1::-1], (1, 0, 2))[:, :, None, :]
    ps_im = jnp.transpose(pw_im[CHUNK - 1::-1], (1, 0, 2))[:, :, None, :]
    bt_re = jnp.transpose(bb_re, (0, 2, 1))[:, None]
    bt_im = jnp.transpose(bb_im, (0, 2, 1))[:, None]
    ws = jnp.concatenate([ps_re * bt_re - ps_im * bt_im, ps_re * bt_im + ps_im * bt_re],
                         axis=-1).reshape(N_GROUPS, MXU_K, 2 * STATE_DIM)
    pt_re = jnp.transpose(pw_re, (1, 2, 0))
    pt_im = jnp.transpose(pw_im, (1, 2, 0))
    ct_re = jnp.transpose(c_re, (0, 2, 1))[:, :, None, :]
    ct_im = jnp.transpose(c_im, (0, 2, 1))[:, :, None, :]

    def ca(t0):
        p_r = pt_re[:, :, t0:t0 + CHUNK, None]
        p_i = pt_im[:, :, t0:t0 + CHUNK, None]
        return ((ct_re * p_r - ct_im * p_i).reshape(N_GROUPS, STATE_DIM, MXU_K),
                (ct_re * p_i + ct_im * p_r).reshape(N_GROUPS, STATE_DIM, MXU_K))

    ca0_re, ca0_im = ca(0)
    ca1_re, ca1_im = ca(1)
    hi = lax.Precision.HIGHEST
    kc = (jnp.einsum('gph,gpx->ghx', bb_re, ca0_re, precision=hi)
          - jnp.einsum('gph,gpx->ghx', bb_im, ca0_im, precision=hi))
    wy = jnp.concatenate([ca1_re, -ca1_im], axis=1)
    a1 = jnp.concatenate([pw_re[CHUNK], pw_re[CHUNK]], axis=-1)
    a2 = jnp.concatenate([-pw_im[CHUNK], pw_im[CHUNK]], axis=-1)
    return ws.astype(jnp.bfloat16), kc, wy.astype(jnp.bfloat16), a1, a2


def kernel(x_prompt, x_sample, state_ssm_re, state_ssm_im, state_conv, norm_mix, norm_mlp, norm_final,
           ssm_a_re, ssm_a_im, ssm_log_dt, ssm_b_re, ssm_b_im, ssm_c_re, ssm_c_im, ssm_d, ssm_w_glu,
           conv_w_in, conv_w, conv_w_out, mlp_w_up, mlp_w_down):
    batch, seq, _ = x_prompt.shape
    dec = x_sample.shape[0]
    n_ssm = ssm_a_re.shape[0]
    n_conv = conv_w_in.shape[0]

    g_mix = norm_mix.reshape(DEPTH, 1, D_MODEL)
    g_mlp = norm_mlp.reshape(DEPTH, 1, D_MODEL)
    g_final = norm_final.reshape(1, D_MODEL)
    d_skip = ssm_d.reshape(n_ssm, 1, D_MODEL)
    step_params, chunk_params = [], []
    for j in range(n_ssm):
        disc = _s5_discretise(ssm_a_re[j], ssm_a_im[j], ssm_log_dt[j], ssm_b_re[j], ssm_b_im[j])
        step_params.append(_s5_step_params(*disc, ssm_c_re[j], ssm_c_im[j]))
        chunk_params.append(_s5_chunk_params(*disc, ssm_c_re[j], ssm_c_im[j]))
    h0_re = state_ssm_re.reshape(n_ssm, dec, N_STATE)
    h0_im = state_ssm_im.reshape(n_ssm, dec, N_STATE)
    buf_s = jnp.transpose(state_conv, (0, 2, 1, 3)).reshape(n_conv, HIST * dec, D_MODEL)

    xp = x_prompt
    xs = x_sample.reshape(dec, D_MODEL)
    pre, pim, sre, sim, pbuf, sbuf = [], [], [], [], [], []
    for i in range(DEPTH):
        j = i // 2
        if i % 2 == 0:
            xp, a, b = _s5_prompt(xp, g_mix, chunk_params[j], d_skip, ssm_w_glu,
                                  layer=i, mixer=j, batch=batch, seq=seq)
            xs, c, d = _s5_sample(xs, h0_re, h0_im, g_mix, step_params[j], d_skip, ssm_w_glu, layer=i, mixer=j)
            pre.append(a), pim.append(b), sre.append(c), sim.append(d)
        else:
            xp, xs, a, b = _conv_mixer(xp, xs, buf_s, g_mix, conv_w_in, conv_w, conv_w_out,
                                       layer=i, mixer=j, batch=batch, seq=seq)
            pbuf.append(a), sbuf.append(b)
        xp, xs = _mlp(xp, xs, g_mlp, mlp_w_up, mlp_w_down, g_final, layer=i, batch=batch, seq=seq,
                      final_norm=(i == DEPTH - 1))

    def states(parts, n):
        return jnp.stack(parts).reshape(n_ssm, n, N_GROUPS, STATE_DIM)

    def bufs(parts, n):
        return jnp.transpose(jnp.stack(parts).reshape(n_conv, HIST, n, D_MODEL), (0, 2, 1, 3))

    return (xp, xs.reshape(dec, 1, D_MODEL), states(pre, batch), states(pim, batch), bufs(pbuf, batch),
            states(sre, dec), states(sim, dec), bufs(sbuf, dec))
```

```python
import functools

import jax
import jax.numpy as jnp
from jax import lax
from jax.experimental import pallas as pl
from jax.experimental.pallas import tpu as pltpu

D_MODEL = 1024
DEPTH = 4
GROUP_SIZE = 16
N_GROUPS = D_MODEL // GROUP_SIZE
STATE_DIM = 64
N_STATE = N_GROUPS * STATE_DIM
CONV_WIDTH = 3
HIST = CONV_WIDTH - 1
D_FF = 4 * D_MODEL
RMS_EPS = 1e-6

LANES = 128
MXU_K = 256
N_SLABS = D_MODEL // MXU_K
SLAB_STATE = N_STATE // N_SLABS
CHUNK = MXU_K // GROUP_SIZE
W_CHUNK = 512
VMEM_LIMIT = 58 * 1024 * 1024
ROW_TILE = 512
S5_TILE = 1024


def _rms_norm(x, g):
    ms = jnp.mean(x * x, axis=-1, keepdims=True)
    return x * lax.rsqrt(ms + RMS_EPS) * g


def _const_spec(shape, index=None):
    index = (0,) * len(shape) if index is None else index
    return pl.BlockSpec(shape, lambda i: index, pipeline_mode=pl.Buffered(1))


def _rows_spec(m, first_step, n_tiles):
    return pl.BlockSpec((m, D_MODEL), lambda i: (jnp.clip(i - first_step, 0, n_tiles - 1), 0))


def _natural_spec(batch, steps, first_step, n_tiles):
    return pl.BlockSpec((batch, steps, D_MODEL),
                        lambda i: (0, jnp.clip(i - first_step, 0, n_tiles - 1), 0))


def _col_chunk_spec(layer, n_rows, n_chunks):
    return pl.BlockSpec((None, n_rows, W_CHUNK), lambda i: (layer, 0, jnp.minimum(i, n_chunks - 1)))


def _row_chunk_spec(layer, n_cols, n_chunks):
    return pl.BlockSpec((None, W_CHUNK, n_cols), lambda i: (layer, jnp.minimum(i, n_chunks - 1), 0))


def _to_time_major(x_ref, slab_ref):
    batch, steps, _ = x_ref.shape
    for c in range(D_MODEL // LANES):
        for b in range(batch):
            slab_ref[c, pl.ds(b, steps, stride=batch), :] = x_ref[b, :, c * LANES:(c + 1) * LANES]
    return jnp.concatenate([slab_ref[c] for c in range(D_MODEL // LANES)], axis=1)


def _from_time_major(y, o_ref, slab_ref):
    batch, steps, _ = o_ref.shape
    for c in range(D_MODEL // LANES):
        slab_ref[c] = y[:, c * LANES:(c + 1) * LANES]
        for b in range(batch):
            o_ref[b, :, c * LANES:(c + 1) * LANES] = slab_ref[c, pl.ds(b, steps, stride=batch), :]


def _glu_tail(x, yb, wglu_ref):
    n_half = D_MODEL // W_CHUNK
    outs = []
    for c in range(n_half):
        z_lin = jnp.dot(yb, wglu_ref[c], preferred_element_type=jnp.float32)
        z_gate = jnp.dot(yb, wglu_ref[n_half + c], preferred_element_type=jnp.float32)
        outs.append(z_lin * jax.nn.sigmoid(z_gate))
    return x + jnp.concatenate(outs, axis=1)


def _s5_step_rows(x, hre_ref, him_ref, g_ref, are_ref, aim_ref, wb_ref, wc_ref, d_ref, wglu_ref,
                  bu_ref, hb_ref, y_ref):
    u = _rms_norm(x, g_ref[...])
    ub = u.astype(jnp.bfloat16)
    for s in range(N_SLABS):
        ch = slice(s * MXU_K, (s + 1) * MXU_K)
        bu_ref[...] = jnp.dot(ub[:, ch], wb_ref[s], preferred_element_type=jnp.float32)
        for l0 in range(0, SLAB_STATE, LANES):
            re_l = slice(l0, l0 + LANES)
            im_l = slice(SLAB_STATE + l0, SLAB_STATE + l0 + LANES)
            st_l = slice(s * SLAB_STATE + l0, s * SLAB_STATE + l0 + LANES)
            a_r, a_i = are_ref[:, st_l], aim_ref[:, st_l]
            h_r, h_i = hre_ref[:, st_l], him_ref[:, st_l]
            n_r = a_r * h_r - a_i * h_i + bu_ref[:, re_l]
            n_i = a_r * h_i + a_i * h_r + bu_ref[:, im_l]
            hre_ref[:, st_l] = n_r
            him_ref[:, st_l] = n_i
            hb_ref[:, re_l] = n_r.astype(jnp.bfloat16)
            hb_ref[:, im_l] = n_i.astype(jnp.bfloat16)
        ys = jnp.dot(hb_ref[...], wc_ref[s], preferred_element_type=jnp.float32)
        ys = ys + d_ref[:, ch] * u[:, ch]
        y_ref[:, ch] = jax.nn.gelu(ys).astype(jnp.bfloat16)
    return _glu_tail(x, y_ref[...], wglu_ref)


def _s5_sample_kernel(xs_ref, h0re_ref, h0im_ref, g_ref, are_ref, aim_ref, wb_ref, wc_ref, d_ref, wglu_f32_ref,
                      os_ref, sre_ref, sim_ref, wglu_ref, bu_ref, hb_ref, y_ref, *, n_cast):
    i = pl.program_id(0)

    @pl.when(i < n_cast)
    def _():
        wglu_ref[i] = wglu_f32_ref[...].astype(jnp.bfloat16)

    @pl.when(i == n_cast)
    def _():
        sre_ref[...] = h0re_ref[...]
        sim_ref[...] = h0im_ref[...]
        os_ref[...] = _s5_step_rows(xs_ref[...], sre_ref, sim_ref, g_ref, are_ref, aim_ref, wb_ref, wc_ref,
                                    d_ref, wglu_ref, bu_ref, hb_ref, y_ref)


def _s5_sample(xs, h0_re, h0_im, gain, step_params, d_skip, w_glu, *, layer, mixer):
    a_re, a_im, wb, wc = step_params
    dec = xs.shape[0]
    n_cast = (2 * D_MODEL) // W_CHUNK
    return pl.pallas_call(
        functools.partial(_s5_sample_kernel, n_cast=n_cast),
        grid=(n_cast + 1,),
        in_specs=[
            _const_spec((dec, D_MODEL)),
            _const_spec((None, dec, N_STATE), (mixer, 0, 0)), _const_spec((None, dec, N_STATE), (mixer, 0, 0)),
            _const_spec((None, 1, D_MODEL), (layer, 0, 0)),
            _const_spec((1, N_STATE)), _const_spec((1, N_STATE)),
            _const_spec((N_SLABS, MXU_K, 2 * SLAB_STATE)),
            _const_spec((N_SLABS, 2 * SLAB_STATE, MXU_K)),
            _const_spec((None, 1, D_MODEL), (mixer, 0, 0)),
            _col_chunk_spec(mixer, D_MODEL, n_cast),
        ],
        out_specs=[
            pl.BlockSpec((dec, D_MODEL), lambda i: (0, 0)),
            pl.BlockSpec((dec, N_STATE), lambda i: (0, 0)), pl.BlockSpec((dec, N_STATE), lambda i: (0, 0)),
        ],
        out_shape=[
            jax.ShapeDtypeStruct((dec, D_MODEL), jnp.float32),
            jax.ShapeDtypeStruct((dec, N_STATE), jnp.float32), jax.ShapeDtypeStruct((dec, N_STATE), jnp.float32),
        ],
        scratch_shapes=[
            pltpu.VMEM((n_cast, D_MODEL, W_CHUNK), jnp.bfloat16),
            pltpu.VMEM((dec, 2 * SLAB_STATE), jnp.float32),
            pltpu.VMEM((dec, 2 * SLAB_STATE), jnp.bfloat16),
            pltpu.VMEM((dec, D_MODEL), jnp.bfloat16),
        ],
        compiler_params=pltpu.CompilerParams(dimension_semantics=("arbitrary",), vmem_limit_bytes=VMEM_LIMIT),
        name="s5_sample",
    )(xs, h0_re, h0_im, gain, a_re, a_im, wb, wc, d_skip, w_glu)


def _block_transpose(sets, lane_block):
    sets = [list(vs) for vs in sets]
    n = len(sets[0])
    d = 1
    while d < n:
        upper = (lane_block & d) != 0
        for vs in sets:
            for i in range(n):
                if i & d == 0:
                    a, b = vs[i], vs[i + d]
                    vs[i] = jnp.where(upper, pltpu.roll(b, GROUP_SIZE * d, 1), a)
                    vs[i + d] = jnp.where(upper, b, pltpu.roll(a, LANES - GROUP_SIZE * d, 1))
        d *= 2
    return sets


def _s5_chunk_rows(x, hst_ref, g_ref, ws_ref, wt_ref, wy_ref, a1_ref, a2_ref, y_ref,
                   lhs_ref, sc_ref, hp_ref, yg_ref, *, batch):
    m = x.shape[0]
    n_chunks = m // (CHUNK * batch)
    blocks = LANES // GROUP_SIZE
    n_cols = D_MODEL // LANES
    n_halves = CHUNK // blocks
    pass_chunks = min(n_chunks, 4)
    pass_rows = pass_chunks * batch
    u = _rms_norm(x, g_ref[...])
    lane_block = lax.broadcasted_iota(jnp.int32, (pass_rows, LANES), 1) // GROUP_SIZE

    def step_rows(c, s):
        r0 = (c * CHUNK + s) * batch
        return slice(r0, r0 + batch)

    for c0 in range(0, n_chunks, pass_chunks):
        prow = slice(c0 * batch, c0 * batch + pass_rows)
        sets = [[jnp.concatenate([u[step_rows(c0 + c, k * blocks + s), q * LANES:(q + 1) * LANES]
                                  for c in range(pass_chunks)], axis=0).astype(jnp.bfloat16)
                 for s in range(blocks)]
                for q in range(n_cols) for k in range(n_halves)]
        w = _block_transpose(sets, lane_block)
        for q in range(n_cols):
            for j in range(blocks):
                lhs = jnp.concatenate([w[q * n_halves + k][j] for k in range(n_halves)], axis=1)
                lhs_ref[q * blocks + j, prow, :] = lhs
    for g in range(N_GROUPS):
        sc_ref[g] = jnp.dot(lhs_ref[g], ws_ref[g], preferred_element_type=jnp.float32)
    for g in range(N_GROUPS):
        a1 = a1_ref[g:g + 1, :]
        a2 = a2_ref[g:g + 1, :]
        sc = sc_ref[g]
        sc_sw = pltpu.roll(sc, STATE_DIM, 1)
        h = hst_ref[g]
        hs = pltpu.roll(h, STATE_DIM, 1)
        h_prev = []
        for c in range(n_chunks):
            h_prev.append(h)
            rows_c = slice(c * batch, (c + 1) * batch)
            h, hs = a1 * h + a2 * hs + sc[rows_c], a1 * hs - a2 * h + sc_sw[rows_c]
        hst_ref[g] = h
        hp_ref[g] = jnp.concatenate(h_prev, axis=0).astype(jnp.bfloat16)
    for g in range(N_GROUPS):
        yg_ref[g] = (jnp.dot(lhs_ref[g], wt_ref[g], preferred_element_type=jnp.float32)
                     + jnp.dot(hp_ref[g], wy_ref[g], preferred_element_type=jnp.float32)).astype(jnp.bfloat16)
    for c0 in range(0, n_chunks, pass_chunks):
        prow = slice(c0 * batch, c0 * batch + pass_rows)
        sets = [[yg_ref[q * blocks + j, prow, k * LANES:(k + 1) * LANES] for j in range(blocks)]
                for q in range(n_cols) for k in range(n_halves)]
        z = _block_transpose(sets, lane_block)
        for q in range(n_cols):
            for k in range(n_halves):
                for s in range(blocks):
                    zs = z[q * n_halves + k][s].astype(jnp.float32)
                    for c in range(pass_chunks):
                        y_ref[step_rows(c0 + c, k * blocks + s), q * LANES:(q + 1) * LANES] = (
                            zs[c * batch:(c + 1) * batch])


def _s5_core_kernel(xp_ref, g_ref, ws_ref, kc_ref, wy_ref, a1_ref, a2_ref,
                    y_ref, pre_ref, pim_ref, wt_ref, hst_ref, lhs_ref, sc_ref, hp_ref, yg_ref,
                    *maybe_slab_ref, n_tiles, batch):
    i = pl.program_id(0)

    @pl.when(i == 0)
    def _():
        hst_ref[...] = jnp.zeros_like(hst_ref)
        lane = lax.broadcasted_iota(jnp.int32, (GROUP_SIZE, MXU_K), 1)

        def build(g, carry):
            kc = kc_ref[g]
            for s in range(CHUNK):
                blk = kc if s == 0 else jnp.where(lane >= s * GROUP_SIZE, pltpu.roll(kc, s * GROUP_SIZE, 1), 0.0)
                wt_ref[g, s * GROUP_SIZE:(s + 1) * GROUP_SIZE, :] = blk.astype(jnp.bfloat16)
            return carry

        lax.fori_loop(0, N_GROUPS, build, 0)

    x = _to_time_major(xp_ref, maybe_slab_ref[0]) if maybe_slab_ref else xp_ref[...]
    _s5_chunk_rows(x, hst_ref, g_ref, ws_ref, wt_ref, wy_ref, a1_ref, a2_ref,
                   y_ref, lhs_ref, sc_ref, hp_ref, yg_ref, batch=batch)

    @pl.when(i == n_tiles - 1)
    def _():
        low = lax.broadcasted_iota(jnp.int32, (batch, LANES), 1) < STATE_DIM
        for g in range(0, N_GROUPS, 2):
            h0, h1 = hst_ref[g], hst_ref[g + 1]
            cols = slice(g * STATE_DIM, (g + 2) * STATE_DIM)
            pre_ref[:, cols] = jnp.where(low, h0, pltpu.roll(h1, STATE_DIM, 1))
            pim_ref[:, cols] = jnp.where(low, pltpu.roll(h0, STATE_DIM, 1), h1)


def _s5_prompt(xp, gain, chunk_params, d_skip, w_glu, *, layer, mixer, batch, seq):
    natural = xp.ndim == 3
    ws, kc, wy, a1, a2 = chunk_params
    m = S5_TILE
    steps = m // batch
    n_tiles = (seq * batch) // m
    n_cast = (2 * D_MODEL) // W_CHUNK

    def xp_spec(first_step):
        return (_natural_spec(batch, steps, first_step, n_tiles) if natural
                else _rows_spec(m, first_step, n_tiles))

    slab = [pltpu.VMEM((D_MODEL // LANES, m, LANES), jnp.float32)] if natural else []
    scratch = [
        pltpu.VMEM((N_GROUPS, MXU_K, MXU_K), jnp.bfloat16),
        pltpu.VMEM((N_GROUPS, batch, 2 * STATE_DIM), jnp.float32),
        pltpu.VMEM((N_GROUPS, m // CHUNK, MXU_K), jnp.bfloat16),
        pltpu.VMEM((N_GROUPS, m // CHUNK, 2 * STATE_DIM), jnp.float32),
        pltpu.VMEM((N_GROUPS, m // CHUNK, 2 * STATE_DIM), jnp.bfloat16),
        pltpu.VMEM((N_GROUPS, m // CHUNK, MXU_K), jnp.bfloat16),
    ]
    y, pre, pim = pl.pallas_call(
        functools.partial(_s5_core_kernel, n_tiles=n_tiles, batch=batch),
        grid=(n_tiles,),
        in_specs=[
            xp_spec(0),
            _const_spec((None, 1, D_MODEL), (layer, 0, 0)),
            _const_spec((N_GROUPS, MXU_K, 2 * STATE_DIM)),
            _const_spec((N_GROUPS, GROUP_SIZE, MXU_K)),
            _const_spec((N_GROUPS, 2 * STATE_DIM, MXU_K)),
            _const_spec((N_GROUPS, 2 * STATE_DIM)), _const_spec((N_GROUPS, 2 * STATE_DIM)),
        ],
        out_specs=[
            _rows_spec(m, 0, n_tiles),
            pl.BlockSpec((batch, N_STATE), lambda i: (0, 0)), pl.BlockSpec((batch, N_STATE), lambda i: (0, 0)),
        ],
        out_shape=[
            jax.ShapeDtypeStruct((seq * batch, D_MODEL), jnp.float32),
            jax.ShapeDtypeStruct((batch, N_STATE), jnp.float32), jax.ShapeDtypeStruct((batch, N_STATE), jnp.float32),
        ],
        scratch_shapes=scratch + slab,
        compiler_params=pltpu.CompilerParams(dimension_semantics=("arbitrary",), vmem_limit_bytes=VMEM_LIMIT),
        name="s5_core",
    )(xp, gain, ws, kc, wy, a1, a2)
    out = pl.pallas_call(
        functools.partial(_glu_kernel, n_cast=n_cast),
        grid=(n_cast + n_tiles,),
        in_specs=[xp_spec(n_cast), _rows_spec(m, n_cast, n_tiles),
                  _const_spec((None, 1, D_MODEL), (layer, 0, 0)), _const_spec((None, 1, D_MODEL), (mixer, 0, 0)),
                  _col_chunk_spec(mixer, D_MODEL, n_cast)],
        out_specs=_rows_spec(m, n_cast, n_tiles),
        out_shape=jax.ShapeDtypeStruct((seq * batch, D_MODEL), jnp.float32),
        scratch_shapes=[pltpu.VMEM((n_cast, D_MODEL, W_CHUNK), jnp.bfloat16)] + slab,
        compiler_params=pltpu.CompilerParams(dimension_semantics=("arbitrary",), vmem_limit_bytes=VMEM_LIMIT),
        name="s5_glu",
    )(xp, y, gain, d_skip, w_glu)
    return out, pre, pim


def _glu_kernel(xp_ref, y_ref, g_ref, d_ref, wglu_f32_ref, o_ref, wglu_ref, *maybe_slab_ref, n_cast):
    i = pl.program_id(0)

    @pl.when(i < n_cast)
    def _():
        wglu_ref[i] = wglu_f32_ref[...].astype(jnp.bfloat16)

    @pl.when(i >= n_cast)
    def _():
        x = _to_time_major(xp_ref, maybe_slab_ref[0]) if maybe_slab_ref else xp_ref[...]
        ys = y_ref[...] + d_ref[...] * _rms_norm(x, g_ref[...])
        o_ref[...] = _glu_tail(x, jax.nn.gelu(ys).astype(jnp.bfloat16), wglu_ref)


def _conv_rows(x, g_ref, win_ref, cw_ref, wout_ref, cvx_ref, *, rows_per_step):
    br = rows_per_step
    m = x.shape[0]
    hist = HIST * br
    n_d = D_MODEL // W_CHUNK
    ub = _rms_norm(x, g_ref[...]).astype(jnp.bfloat16)
    bcv = [jnp.dot(ub, win_ref[c], preferred_element_type=jnp.float32) for c in range(3 * n_d)]
    outs = 0.0
    for c in range(n_d):
        cols = slice(c * W_CHUNK, (c + 1) * W_CHUNK)
        cv = bcv[n_d + c] * bcv[2 * n_d + c]
        cvx_ref[hist:hist + m, cols] = cv
        y = cw_ref[CONV_WIDTH - 1:CONV_WIDTH, cols] * cv
        for k in range(CONV_WIDTH - 1):
            y = y + cw_ref[k:k + 1, cols] * cvx_ref[k * br:k * br + m, cols]
        gated = (bcv[c] * y).astype(jnp.bfloat16)
        outs = outs + jnp.dot(gated, wout_ref[c], preferred_element_type=jnp.float32)
    cvx_ref[0:hist, :] = cvx_ref[m:m + hist, :]
    return x + outs


def _conv_kernel(xp_ref, xs_ref, bufs_ref, g_ref, win_f32_ref, cw_ref, wout_f32_ref,
                 op_ref, os_ref, nbufp_ref, nbufs_ref, win_ref, wout_ref, cvx_ref, *, n_cast, n_tiles, batch):
    i = pl.program_id(0)
    dec = xs_ref.shape[0]
    body = functools.partial(_conv_rows, g_ref=g_ref, win_ref=win_ref, cw_ref=cw_ref, wout_ref=wout_ref,
                             cvx_ref=cvx_ref)

    @pl.when(i < n_cast)
    def _():
        win_ref[i] = win_f32_ref[...].astype(jnp.bfloat16)

    @pl.when(i < wout_ref.shape[0])
    def _():
        wout_ref[i] = wout_f32_ref[...].astype(jnp.bfloat16)

    @pl.when(i == n_cast)
    def _():
        cvx_ref[0:HIST * batch, :] = jnp.zeros((HIST * batch, D_MODEL), jnp.float32)

    @pl.when(jnp.logical_and(i >= n_cast, i < n_cast + n_tiles))
    def _():
        op_ref[...] = body(xp_ref[...], rows_per_step=batch)

    @pl.when(i == n_cast + n_tiles - 1)
    def _():
        nbufp_ref[...] = cvx_ref[0:HIST * batch, :]

    @pl.when(i == n_cast + n_tiles)
    def _():
        cvx_ref[0:HIST * dec, :] = bufs_ref[...]
        os_ref[...] = body(xs_ref[...], rows_per_step=dec)
        nbufs_ref[...] = cvx_ref[0:HIST * dec, :]


def _conv_mixer(xp, xs, buf_s, gain, w_in, conv_w, w_out, *, layer, mixer, batch, seq):
    dec = xs.shape[0]
    m = ROW_TILE
    n_tiles = (seq * batch) // m
    n_cast = (3 * D_MODEL) // W_CHUNK
    n_out = D_MODEL // W_CHUNK
    kern = functools.partial(_conv_kernel, n_cast=n_cast, n_tiles=n_tiles, batch=batch)
    return pl.pallas_call(
        kern,
        grid=(n_cast + n_tiles + 1,),
        in_specs=[
            _rows_spec(m, n_cast, n_tiles),
            _const_spec((dec, D_MODEL)),
            _const_spec((None, HIST * dec, D_MODEL), (mixer, 0, 0)),
            _const_spec((None, 1, D_MODEL), (layer, 0, 0)),
            _col_chunk_spec(mixer, D_MODEL, n_cast),
            _const_spec((None, CONV_WIDTH, D_MODEL), (mixer, 0, 0)),
            _row_chunk_spec(mixer, D_MODEL, n_out),
        ],
        out_specs=[
            _rows_spec(m, n_cast, n_tiles),
            pl.BlockSpec((dec, D_MODEL), lambda i: (0, 0)),
            pl.BlockSpec((HIST * batch, D_MODEL), lambda i: (0, 0)),
            pl.BlockSpec((HIST * dec, D_MODEL), lambda i: (0, 0)),
        ],
        out_shape=[
            jax.ShapeDtypeStruct((seq * batch, D_MODEL), jnp.float32),
            jax.ShapeDtypeStruct((dec, D_MODEL), jnp.float32),
            jax.ShapeDtypeStruct((HIST * batch, D_MODEL), jnp.float32),
            jax.ShapeDtypeStruct((HIST * dec, D_MODEL), jnp.float32),
        ],
        scratch_shapes=[
            pltpu.VMEM((n_cast, D_MODEL, W_CHUNK), jnp.bfloat16),
            pltpu.VMEM((n_out, W_CHUNK, D_MODEL), jnp.bfloat16),
            pltpu.VMEM((max(HIST * batch + m, (HIST + 1) * dec), D_MODEL), jnp.float32),
        ],
        compiler_params=pltpu.CompilerParams(dimension_semantics=("arbitrary",), vmem_limit_bytes=VMEM_LIMIT),
        name="conv_mixer",
    )(xp, xs, buf_s, gain, w_in, conv_w, w_out)


def _mlp_chunk(ub, up_ref, down_ref, c):
    h = jnp.dot(ub, up_ref[c], preferred_element_type=jnp.float32)
    h = jnp.maximum(h, 0.0)
    h = (h * h).astype(jnp.bfloat16)
    return jnp.dot(h, down_ref[c], preferred_element_type=jnp.float32)


def _mlp_rows(x, g_ref, up_ref, down_ref, gf_ref, *, final_norm):
    ub = _rms_norm(x, g_ref[...]).astype(jnp.bfloat16)
    acc = x
    for c in range(D_FF // W_CHUNK):
        acc = acc + _mlp_chunk(ub, up_ref, down_ref, c)
    if final_norm:
        acc = _rms_norm(acc, gf_ref[...])
    return acc


def _mlp_kernel(xp_ref, xs_ref, g_ref, up_f32_ref, down_f32_ref, gf_ref, op_ref, os_ref,
                up_ref, down_ref, ub0_ref, acc0_ref, *maybe_slab_ref, n_cast, n_tiles, final_norm):
    i = pl.program_id(0)
    body = functools.partial(_mlp_rows, g_ref=g_ref, up_ref=up_ref, down_ref=down_ref, gf_ref=gf_ref,
                             final_norm=final_norm)

    def write_prompt(y):
        if maybe_slab_ref:
            _from_time_major(y, op_ref, maybe_slab_ref[0])
        else:
            op_ref[...] = y

    @pl.when(i == 0)
    def _():
        x = xp_ref[...]
        ub0_ref[...] = _rms_norm(x, g_ref[...]).astype(jnp.bfloat16)
        acc0_ref[...] = x

    @pl.when(i < n_cast)
    def _():
        up_ref[i] = up_f32_ref[...].astype(jnp.bfloat16)
        down_ref[i] = down_f32_ref[...].astype(jnp.bfloat16)
        acc0_ref[...] += _mlp_chunk(ub0_ref[...], up_ref, down_ref, i)

    @pl.when(i == n_cast - 1)
    def _():
        y = acc0_ref[...]
        write_prompt(_rms_norm(y, gf_ref[...]) if final_norm else y)

    @pl.when(jnp.logical_and(i >= n_cast, i < n_cast + n_tiles - 1))
    def _():
        write_prompt(body(xp_ref[...]))

    @pl.when(i == n_cast + n_tiles - 1)
    def _():
        os_ref[...] = body(xs_ref[...])


def _mlp(xp, xs, gain, w_up, w_down, gain_final, *, layer, batch, seq, final_norm):
    dec = xs.shape[0]
    m = ROW_TILE
    steps = m // batch
    n_tiles = (seq * batch) // m
    n_cast = D_FF // W_CHUNK
    scratch = [
        pltpu.VMEM((n_cast, D_MODEL, W_CHUNK), jnp.bfloat16),
        pltpu.VMEM((n_cast, W_CHUNK, D_MODEL), jnp.bfloat16),
        pltpu.VMEM((m, D_MODEL), jnp.bfloat16),
        pltpu.VMEM((m, D_MODEL), jnp.float32),
    ]
    first = n_cast - 1
    if final_norm:
        scratch.append(pltpu.VMEM((D_MODEL // LANES, m, LANES), jnp.float32))
        op_spec = _natural_spec(batch, steps, first, n_tiles)
        op_shape = jax.ShapeDtypeStruct((batch, seq, D_MODEL), jnp.float32)
    else:
        op_spec = _rows_spec(m, first, n_tiles)
        op_shape = jax.ShapeDtypeStruct((seq * batch, D_MODEL), jnp.float32)
    kern = functools.partial(_mlp_kernel, n_cast=n_cast, n_tiles=n_tiles, final_norm=final_norm)
    return pl.pallas_call(
        kern,
        grid=(n_cast + n_tiles,),
        in_specs=[
            _rows_spec(m, first, n_tiles),
            _const_spec((dec, D_MODEL)),
            _const_spec((None, 1, D_MODEL), (layer, 0, 0)),
            _col_chunk_spec(layer, D_MODEL, n_cast),
            _row_chunk_spec(layer, D_MODEL, n_cast),
            _const_spec((1, D_MODEL)),
        ],
        out_specs=[op_spec, pl.BlockSpec((dec, D_MODEL), lambda i: (0, 0))],
        out_shape=[op_shape, jax.ShapeDtypeStruct((dec, D_MODEL), jnp.float32)],
        scratch_shapes=scratch,
        compiler_params=pltpu.CompilerParams(dimension_semantics=("arbitrary",), vmem_limit_bytes=VMEM_LIMIT),
        name="mlp",
    )(xp, xs, gain, w_up, w_down, gain_final)


def _s5_discretise(a_re, a_im, log_dt, b_re, b_im):
    dt = jnp.exp(log_dt)[:, None]
    k = jnp.arange(CHUNK + 1, dtype=jnp.float32)[:, None, None]
    mag = jnp.exp(k * (a_re * dt))
    pw_re = mag * jnp.cos(k * (a_im * dt))
    pw_im = mag * jnp.sin(k * (a_im * dt))
    ab_re, ab_im = pw_re[1], pw_im[1]
    den = a_re * a_re + a_im * a_im
    q_re = ((ab_re - 1.0) * a_re + ab_im * a_im) / den
    q_im = (ab_im * a_re - (ab_re - 1.0) * a_im) / den
    bb_re = q_re[..., None] * b_re - q_im[..., None] * b_im
    bb_im = q_re[..., None] * b_im + q_im[..., None] * b_re
    return pw_re, pw_im, bb_re, bb_im


def _s5_step_params(pw_re, pw_im, bb_re, bb_im, c_re, c_im):
    gps = MXU_K // GROUP_SIZE
    eye = jnp.eye(gps, dtype=jnp.float32)

    def pack_b(w):
        w = w.reshape(N_SLABS, gps, STATE_DIM, GROUP_SIZE)
        return jnp.einsum('sgph,gk->sghkp', w, eye).reshape(N_SLABS, MXU_K, SLAB_STATE)

    def pack_c(w):
        w = w.reshape(N_SLABS, gps, GROUP_SIZE, STATE_DIM)
        return jnp.einsum('sghp,gk->sgpkh', w, eye).reshape(N_SLABS, SLAB_STATE, MXU_K)

    wb = jnp.concatenate([pack_b(bb_re), pack_b(bb_im)], axis=2).astype(jnp.bfloat16)
    wc = jnp.concatenate([pack_c(c_re), pack_c(-c_im)], axis=1).astype(jnp.bfloat16)
    return pw_re[1].reshape(1, N_STATE), pw_im[1].reshape(1, N_STATE), wb, wc


def _s5_chunk_params(pw_re, pw_im, bb_re, bb_im, c_re, c_im):
    ps_re = jnp.transpose(pw_re[CHUNK - 1::-1], (1, 0, 2))[:, :, None, :]
    ps_im = jnp.transpose(pw_im[CHUNK - 1::-1], (1, 0, 2))[:, :, None, :]
    bt_re = jnp.transpose(bb_re, (0, 2, 1))[:, None]
    bt_im = jnp.transpose(bb_im, (0, 2, 1))[:, None]
    ws = jnp.concatenate([ps_re * bt_re - ps_im * bt_im, ps_re * bt_im + ps_im * bt_re],
                         axis=-1).reshape(N_GROUPS, MXU_K, 2 * STATE_DIM)
    pt_re = jnp.transpose(pw_re, (1, 2, 0))
    pt_im = jnp.transpose(pw_im, (1, 2, 0))
    ct_re = jnp.transpose(c_re, (0, 2, 1))[:, :, None, :]
    ct_im = jnp.transpose(c_im, (0, 2, 1))[:, :, None, :]

    def ca(t0):
        p_r = pt_re[:, :, t0:t0 + CHUNK, None]
        p_i = pt_im[:, :, t0:t0 + CHUNK, None]
        return ((ct_re * p_r - ct_im * p_i).reshape(N_GROUPS, STATE_DIM, MXU_K),
                (ct_re * p_i + ct_im * p_r).reshape(N_GROUPS, STATE_DIM, MXU_K))

    ca0_re, ca0_im = ca(0)
    ca1_re, ca1_im = ca(1)
    hi = lax.Precision.HIGHEST
    kc = (jnp.einsum('gph,gpx->ghx', bb_re, ca0_re, precision=hi)
          - jnp.einsum('gph,gpx->ghx', bb_im, ca0_im, precision=hi))
    wy = jnp.concatenate([ca1_re, -ca1_im], axis=1)
    a1 = jnp.concatenate([pw_re[CHUNK], pw_re[CHUNK]], axis=-1)
    a2 = jnp.concatenate([-pw_im[CHUNK], pw_im[CHUNK]], axis=-1)
    return ws.astype(jnp.bfloat16), kc, wy.astype(jnp.bfloat16), a1, a2


def kernel(x_prompt, x_sample, state_ssm_re, state_ssm_im, state_conv, norm_mix, norm_mlp, norm_final,
           ssm_a_re, ssm_a_im, ssm_log_dt, ssm_b_re, ssm_b_im, ssm_c_re, ssm_c_im, ssm_d, ssm_w_glu,
           conv_w_in, conv_w, conv_w_out, mlp_w_up, mlp_w_down):
    batch, seq, _ = x_prompt.shape
    dec = x_sample.shape[0]
    n_ssm = ssm_a_re.shape[0]
    n_conv = conv_w_in.shape[0]

    g_mix = norm_mix.reshape(DEPTH, 1, D_MODEL)
    g_mlp = norm_mlp.reshape(DEPTH, 1, D_MODEL)
    g_final = norm_final.reshape(1, D_MODEL)
    d_skip = ssm_d.reshape(n_ssm, 1, D_MODEL)
    step_params, chunk_params = [], []
    for j in range(n_ssm):
        disc = _s5_discretise(ssm_a_re[j], ssm_a_im[j], ssm_log_dt[j], ssm_b_re[j], ssm_b_im[j])
        step_params.append(_s5_step_params(*disc, ssm_c_re[j], ssm_c_im[j]))
        chunk_params.append(_s5_chunk_params(*disc, ssm_c_re[j], ssm_c_im[j]))
    h0_re = state_ssm_re.reshape(n_ssm, dec, N_STATE)
    h0_im = state_ssm_im.reshape(n_ssm, dec, N_STATE)
    buf_s = jnp.transpose(state_conv, (0, 2, 1, 3)).reshape(n_conv, HIST * dec, D_MODEL)

    xp = x_prompt
    xs = x_sample.reshape(dec, D_MODEL)
    pre, pim, sre, sim, pbuf, sbuf = [], [], [], [], [], []
    for i in range(DEPTH):
        j = i // 2
        if i % 2 == 0:
            xp, a, b = _s5_prompt(xp, g_mix, chunk_params[j], d_skip, ssm_w_glu,
                                  layer=i, mixer=j, batch=batch, seq=seq)
            xs, c, d = _s5_sample(xs, h0_re, h0_im, g_mix, step_params[j], d_skip, ssm_w_glu, layer=i, mixer=j)
            pre.append(a), pim.append(b), sre.append(c), sim.append(d)
        else:
            xp, xs, a, b = _conv_mixer(xp, xs, buf_s, g_mix, conv_w_in, conv_w, conv_w_out,
                                       layer=i, mixer=j, batch=batch, seq=seq)
            pbuf.append(a), sbuf.append(b)
        xp, xs = _mlp(xp, xs, g_mlp, mlp_w_up, mlp_w_down, g_final, layer=i, batch=batch, seq=seq,
                      final_norm=(i == DEPTH - 1))

    def states(parts, n):
        return jnp.stack(parts).reshape(n_ssm, n, N_GROUPS, STATE_DIM)

    def bufs(parts, n):
        return jnp.transpose(jnp.stack(parts).reshape(n_conv, HIST, n, D_MODEL), (0, 2, 1, 3))

    return (xp, xs.reshape(dec, 1, D_MODEL), states(pre, batch), states(pim, batch), bufs(pbuf, batch),
            states(sre, dec), states(sim, dec), bufs(sbuf, dec))
```

```python
import functools

import jax
import jax.numpy as jnp
from jax import lax
from jax.experimental import pallas as pl
from jax.experimental.pallas import tpu as pltpu

D_MODEL = 1024
DEPTH = 4
GROUP_SIZE = 16
N_GROUPS = D_MODEL // GROUP_SIZE
STATE_DIM = 64
N_STATE = N_GROUPS * STATE_DIM
CONV_WIDTH = 3
HIST = CONV_WIDTH - 1
D_FF = 4 * D_MODEL
RMS_EPS = 1e-6

LANES = 128
MXU_K = 256
N_SLABS = D_MODEL // MXU_K
SLAB_STATE = N_STATE // N_SLABS
CHUNK = MXU_K // GROUP_SIZE
W_CHUNK = 512
VMEM_LIMIT = 58 * 1024 * 1024
ROW_TILE = 512
S5_TILE = 1024
GLU_ROWS = 256


def _rms_norm(x, g):
    ms = jnp.mean(x * x, axis=-1, keepdims=True)
    return x * lax.rsqrt(ms + RMS_EPS) * g


def _const_spec(shape, index=None):
    index = (0,) * len(shape) if index is None else index
    return pl.BlockSpec(shape, lambda i: index, pipeline_mode=pl.Buffered(1))


def _rows_spec(m, first_step, n_tiles):
    return pl.BlockSpec((m, D_MODEL), lambda i: (jnp.clip(i - first_step, 0, n_tiles - 1), 0))


def _natural_spec(batch, steps, first_step, n_tiles):
    return pl.BlockSpec((batch, steps, D_MODEL),
                        lambda i: (0, jnp.clip(i - first_step, 0, n_tiles - 1), 0))


def _col_chunk_spec(layer, n_rows, n_chunks):
    return pl.BlockSpec((None, n_rows, W_CHUNK), lambda i: (layer, 0, jnp.minimum(i, n_chunks - 1)))


def _row_chunk_spec(layer, n_cols, n_chunks):
    return pl.BlockSpec((None, W_CHUNK, n_cols), lambda i: (layer, jnp.minimum(i, n_chunks - 1), 0))


def _to_time_major(x_ref, slab_ref):
    batch, steps, _ = x_ref.shape
    for c in range(D_MODEL // LANES):
        for b in range(batch):
            slab_ref[c, pl.ds(b, steps, stride=batch), :] = x_ref[b, :, c * LANES:(c + 1) * LANES]
    return jnp.concatenate([slab_ref[c] for c in range(D_MODEL // LANES)], axis=1)


def _from_time_major(y, o_ref, slab_ref):
    batch, steps, _ = o_ref.shape
    for c in range(D_MODEL // LANES):
        slab_ref[c] = y[:, c * LANES:(c + 1) * LANES]
        for b in range(batch):
            o_ref[b, :, c * LANES:(c + 1) * LANES] = slab_ref[c, pl.ds(b, steps, stride=batch), :]


def _glu_tail(x, yb, wglu_ref):
    n_half = D_MODEL // W_CHUNK
    outs = []
    for c in range(n_half):
        z_lin = jnp.dot(yb, wglu_ref[c], preferred_element_type=jnp.float32)
        z_gate = jnp.dot(yb, wglu_ref[n_half + c], preferred_element_type=jnp.float32)
        outs.append(z_lin * jax.nn.sigmoid(z_gate))
    return x + jnp.concatenate(outs, axis=1)


def _s5_step_rows(x, hre_ref, him_ref, g_ref, are_ref, aim_ref, wb_ref, wc_ref, d_ref, wglu_ref,
                  bu_ref, hb_ref, y_ref):
    u = _rms_norm(x, g_ref[...])
    ub = u.astype(jnp.bfloat16)
    for s in range(N_SLABS):
        ch = slice(s * MXU_K, (s + 1) * MXU_K)
        bu_ref[...] = jnp.dot(ub[:, ch], wb_ref[s], preferred_element_type=jnp.float32)
        for l0 in range(0, SLAB_STATE, LANES):
            re_l = slice(l0, l0 + LANES)
            im_l = slice(SLAB_STATE + l0, SLAB_STATE + l0 + LANES)
            st_l = slice(s * SLAB_STATE + l0, s * SLAB_STATE + l0 + LANES)
            a_r, a_i = are_ref[:, st_l], aim_ref[:, st_l]
            h_r, h_i = hre_ref[:, st_l], him_ref[:, st_l]
            n_r = a_r * h_r - a_i * h_i + bu_ref[:, re_l]
            n_i = a_r * h_i + a_i * h_r + bu_ref[:, im_l]
            hre_ref[:, st_l] = n_r
            him_ref[:, st_l] = n_i
            hb_ref[:, re_l] = n_r.astype(jnp.bfloat16)
            hb_ref[:, im_l] = n_i.astype(jnp.bfloat16)
        ys = jnp.dot(hb_ref[...], wc_ref[s], preferred_element_type=jnp.float32)
        ys = ys + d_ref[:, ch] * u[:, ch]
        y_ref[:, ch] = jax.nn.gelu(ys).astype(jnp.bfloat16)
    return _glu_tail(x, y_ref[...], wglu_ref)


def _s5_sample_kernel(xs_ref, h0re_ref, h0im_ref, g_ref, are_ref, aim_ref, wb_ref, wc_ref, d_ref, wglu_f32_ref,
                      os_ref, sre_ref, sim_ref, wglu_ref, bu_ref, hb_ref, y_ref, *, n_cast):
    i = pl.program_id(0)

    @pl.when(i < n_cast)
    def _():
        wglu_ref[i] = wglu_f32_ref[...].astype(jnp.bfloat16)

    @pl.when(i == n_cast)
    def _():
        sre_ref[...] = h0re_ref[...]
        sim_ref[...] = h0im_ref[...]
        os_ref[...] = _s5_step_rows(xs_ref[...], sre_ref, sim_ref, g_ref, are_ref, aim_ref, wb_ref, wc_ref,
                                    d_ref, wglu_ref, bu_ref, hb_ref, y_ref)


def _s5_sample(xs, h0_re, h0_im, gain, step_params, d_skip, w_glu, *, layer, mixer):
    a_re, a_im, wb, wc = step_params
    dec = xs.shape[0]
    n_cast = (2 * D_MODEL) // W_CHUNK
    return pl.pallas_call(
        functools.partial(_s5_sample_kernel, n_cast=n_cast),
        grid=(n_cast + 1,),
        in_specs=[
            _const_spec((dec, D_MODEL)),
            _const_spec((None, dec, N_STATE), (mixer, 0, 0)), _const_spec((None, dec, N_STATE), (mixer, 0, 0)),
            _const_spec((None, 1, D_MODEL), (layer, 0, 0)),
            _const_spec((1, N_STATE)), _const_spec((1, N_STATE)),
            _const_spec((N_SLABS, MXU_K, 2 * SLAB_STATE)),
            _const_spec((N_SLABS, 2 * SLAB_STATE, MXU_K)),
            _const_spec((None, 1, D_MODEL), (mixer, 0, 0)),
            _col_chunk_spec(mixer, D_MODEL, n_cast),
        ],
        out_specs=[
            pl.BlockSpec((dec, D_MODEL), lambda i: (0, 0)),
            pl.BlockSpec((dec, N_STATE), lambda i: (0, 0)), pl.BlockSpec((dec, N_STATE), lambda i: (0, 0)),
        ],
        out_shape=[
            jax.ShapeDtypeStruct((dec, D_MODEL), jnp.float32),
            jax.ShapeDtypeStruct((dec, N_STATE), jnp.float32), jax.ShapeDtypeStruct((dec, N_STATE), jnp.float32),
        ],
        scratch_shapes=[
            pltpu.VMEM((n_cast, D_MODEL, W_CHUNK), jnp.bfloat16),
            pltpu.VMEM((dec, 2 * SLAB_STATE), jnp.float32),
            pltpu.VMEM((dec, 2 * SLAB_STATE), jnp.bfloat16),
            pltpu.VMEM((dec, D_MODEL), jnp.bfloat16),
        ],
        compiler_params=pltpu.CompilerParams(dimension_semantics=("arbitrary",), vmem_limit_bytes=VMEM_LIMIT),
        name="s5_sample",
    )(xs, h0_re, h0_im, gain, a_re, a_im, wb, wc, d_skip, w_glu)


def _block_transpose(sets, lane_block):
    sets = [list(vs) for vs in sets]
    n = len(sets[0])
    d = 1
    while d < n:
        upper = (lane_block & d) != 0
        for vs in sets:
            for i in range(n):
                if i & d == 0:
                    a, b = vs[i], vs[i + d]
                    vs[i] = jnp.where(upper, pltpu.roll(b, GROUP_SIZE * d, 1), a)
                    vs[i + d] = jnp.where(upper, b, pltpu.roll(a, LANES - GROUP_SIZE * d, 1))
        d *= 2
    return sets


def _s5_chunk_rows(x, hst_ref, g_ref, ws_ref, wt_ref, wy_ref, a1_ref, a2_ref, y_ref,
                   lhs_ref, sc_ref, hp_ref, yg_ref, *, batch):
    m = x.shape[0]
    n_chunks = m // (CHUNK * batch)
    blocks = LANES // GROUP_SIZE
    n_cols = D_MODEL // LANES
    n_halves = CHUNK // blocks
    pass_chunks = min(n_chunks, 4)
    pass_rows = pass_chunks * batch
    u = _rms_norm(x, g_ref[...])
    lane_block = lax.broadcasted_iota(jnp.int32, (pass_rows, LANES), 1) // GROUP_SIZE

    def step_rows(c, s):
        r0 = (c * CHUNK + s) * batch
        return slice(r0, r0 + batch)

    for c0 in range(0, n_chunks, pass_chunks):
        prow = slice(c0 * batch, c0 * batch + pass_rows)
        sets = [[jnp.concatenate([u[step_rows(c0 + c, k * blocks + s), q * LANES:(q + 1) * LANES]
                                  for c in range(pass_chunks)], axis=0).astype(jnp.bfloat16)
                 for s in range(blocks)]
                for q in range(n_cols) for k in range(n_halves)]
        w = _block_transpose(sets, lane_block)
        for q in range(n_cols):
            for j in range(blocks):
                lhs = jnp.concatenate([w[q * n_halves + k][j] for k in range(n_halves)], axis=1)
                lhs_ref[q * blocks + j, prow, :] = lhs
    for g in range(N_GROUPS):
        sc_ref[g] = jnp.dot(lhs_ref[g], ws_ref[g], preferred_element_type=jnp.float32)
    for g in range(N_GROUPS):
        a1 = a1_ref[g:g + 1, :]
        a2 = a2_ref[g:g + 1, :]
        sc = sc_ref[g]
        sc_sw = pltpu.roll(sc, STATE_DIM, 1)
        h = hst_ref[g]
        hs = pltpu.roll(h, STATE_DIM, 1)
        h_prev = []
        for c in range(n_chunks):
            h_prev.append(h)
            rows_c = slice(c * batch, (c + 1) * batch)
            h, hs = a1 * h + a2 * hs + sc[rows_c], a1 * hs - a2 * h + sc_sw[rows_c]
        hst_ref[g] = h
        hp_ref[g] = jnp.concatenate(h_prev, axis=0).astype(jnp.bfloat16)
    for g in range(N_GROUPS):
        yg_ref[g] = (jnp.dot(lhs_ref[g], wt_ref[g], preferred_element_type=jnp.float32)
                     + jnp.dot(hp_ref[g], wy_ref[g], preferred_element_type=jnp.float32)).astype(jnp.bfloat16)
    for c0 in range(0, n_chunks, pass_chunks):
        prow = slice(c0 * batch, c0 * batch + pass_rows)
        sets = [[yg_ref[q * blocks + j, prow, k * LANES:(k + 1) * LANES] for j in range(blocks)]
                for q in range(n_cols) for k in range(n_halves)]
        z = _block_transpose(sets, lane_block)
        for q in range(n_cols):
            for k in range(n_halves):
                for s in range(blocks):
                    zs = z[q * n_halves + k][s].astype(jnp.float32)
                    for c in range(pass_chunks):
                        y_ref[step_rows(c0 + c, k * blocks + s), q * LANES:(q + 1) * LANES] = (
                            zs[c * batch:(c + 1) * batch])


def _s5_core_kernel(xp_ref, g_ref, ws_ref, kc_ref, wy_ref, a1_ref, a2_ref,
                    y_ref, pre_ref, pim_ref, wt_ref, hst_ref, lhs_ref, sc_ref, hp_ref, yg_ref,
                    *maybe_slab_ref, n_tiles, batch):
    i = pl.program_id(0)

    @pl.when(i == 0)
    def _():
        hst_ref[...] = jnp.zeros_like(hst_ref)
        lane = lax.broadcasted_iota(jnp.int32, (GROUP_SIZE, MXU_K), 1)

        def build(g, carry):
            kc = kc_ref[g]
            for s in range(CHUNK):
                blk = kc if s == 0 else jnp.where(lane >= s * GROUP_SIZE, pltpu.roll(kc, s * GROUP_SIZE, 1), 0.0)
                wt_ref[g, s * GROUP_SIZE:(s + 1) * GROUP_SIZE, :] = blk.astype(jnp.bfloat16)
            return carry

        lax.fori_loop(0, N_GROUPS, build, 0)

    x = _to_time_major(xp_ref, maybe_slab_ref[0]) if maybe_slab_ref else xp_ref[...]
    _s5_chunk_rows(x, hst_ref, g_ref, ws_ref, wt_ref, wy_ref, a1_ref, a2_ref,
                   y_ref, lhs_ref, sc_ref, hp_ref, yg_ref, batch=batch)

    @pl.when(i == n_tiles - 1)
    def _():
        low = lax.broadcasted_iota(jnp.int32, (batch, LANES), 1) < STATE_DIM
        for g in range(0, N_GROUPS, 2):
            h0, h1 = hst_ref[g], hst_ref[g + 1]
            cols = slice(g * STATE_DIM, (g + 2) * STATE_DIM)
            pre_ref[:, cols] = jnp.where(low, h0, pltpu.roll(h1, STATE_DIM, 1))
            pim_ref[:, cols] = jnp.where(low, pltpu.roll(h0, STATE_DIM, 1), h1)


def _s5_core(xp, gain, chunk_params, *, layer, batch, seq):
    natural = xp.ndim == 3
    ws, kc, wy, a1, a2 = chunk_params
    m = S5_TILE
    n_tiles = (seq * batch) // m

    def xp_spec(first_step):
        return (_natural_spec(batch, m // batch, first_step, n_tiles) if natural
                else _rows_spec(m, first_step, n_tiles))

    slab = [pltpu.VMEM((D_MODEL // LANES, m, LANES), jnp.float32)] if natural else []
    scratch = [
        pltpu.VMEM((N_GROUPS, MXU_K, MXU_K), jnp.bfloat16),
        pltpu.VMEM((N_GROUPS, batch, 2 * STATE_DIM), jnp.float32),
        pltpu.VMEM((N_GROUPS, m // CHUNK, MXU_K), jnp.bfloat16),
        pltpu.VMEM((N_GROUPS, m // CHUNK, 2 * STATE_DIM), jnp.float32),
        pltpu.VMEM((N_GROUPS, m // CHUNK, 2 * STATE_DIM), jnp.bfloat16),
        pltpu.VMEM((N_GROUPS, m // CHUNK, MXU_K), jnp.bfloat16),
    ]
    y, pre, pim = pl.pallas_call(
        functools.partial(_s5_core_kernel, n_tiles=n_tiles, batch=batch),
        grid=(n_tiles,),
        in_specs=[
            xp_spec(0),
            _const_spec((None, 1, D_MODEL), (layer, 0, 0)),
            _const_spec((N_GROUPS, MXU_K, 2 * STATE_DIM)),
            _const_spec((N_GROUPS, GROUP_SIZE, MXU_K)),
            _const_spec((N_GROUPS, 2 * STATE_DIM, MXU_K)),
            _const_spec((N_GROUPS, 2 * STATE_DIM)), _const_spec((N_GROUPS, 2 * STATE_DIM)),
        ],
        out_specs=[
            _rows_spec(m, 0, n_tiles),
            pl.BlockSpec((batch, N_STATE), lambda i: (0, 0)), pl.BlockSpec((batch, N_STATE), lambda i: (0, 0)),
        ],
        out_shape=[
            jax.ShapeDtypeStruct((seq * batch, D_MODEL), jnp.float32),
            jax.ShapeDtypeStruct((batch, N_STATE), jnp.float32), jax.ShapeDtypeStruct((batch, N_STATE), jnp.float32),
        ],
        scratch_shapes=scratch + slab,
        compiler_params=pltpu.CompilerParams(dimension_semantics=("arbitrary",), vmem_limit_bytes=VMEM_LIMIT),
        name="s5_core",
    )(xp, gain, ws, kc, wy, a1, a2)
    return y, pre, pim


def _s5_finish_rows(x, y_ref, g_ref, d_ref, wglu_ref):
    parts = []
    for r0 in range(0, x.shape[0], GLU_ROWS):
        rows = slice(r0, r0 + GLU_ROWS)
        ys = y_ref[rows, :] + d_ref[...] * _rms_norm(x[rows], g_ref[...])
        parts.append(_glu_tail(x[rows], jax.nn.gelu(ys).astype(jnp.bfloat16), wglu_ref))
    return jnp.concatenate(parts, axis=0)


def _conv_rows(x, g_ref, win_ref, cw_ref, wout_ref, cvx_ref, *, rows_per_step):
    br = rows_per_step
    m = x.shape[0]
    hist = HIST * br
    n_d = D_MODEL // W_CHUNK
    ub = _rms_norm(x, g_ref[...]).astype(jnp.bfloat16)
    bcv = [jnp.dot(ub, win_ref[c], preferred_element_type=jnp.float32) for c in range(3 * n_d)]
    outs = 0.0
    for c in range(n_d):
        cols = slice(c * W_CHUNK, (c + 1) * W_CHUNK)
        cv = bcv[n_d + c] * bcv[2 * n_d + c]
        cvx_ref[hist:hist + m, cols] = cv
        y = cw_ref[CONV_WIDTH - 1:CONV_WIDTH, cols] * cv
        for k in range(CONV_WIDTH - 1):
            y = y + cw_ref[k:k + 1, cols] * cvx_ref[k * br:k * br + m, cols]
        gated = (bcv[c] * y).astype(jnp.bfloat16)
        outs = outs + jnp.dot(gated, wout_ref[c], preferred_element_type=jnp.float32)
    cvx_ref[0:hist, :] = cvx_ref[m:m + hist, :]
    return x + outs


def _conv_kernel(xp_ref, xs_ref, bufs_ref, g_ref, win_f32_ref, cw_ref, wout_f32_ref,
                 op_ref, os_ref, nbufp_ref, nbufs_ref, win_ref, wout_ref, cvx_ref, *, n_cast, n_tiles, batch):
    i = pl.program_id(0)
    dec = xs_ref.shape[0]
    body = functools.partial(_conv_rows, g_ref=g_ref, win_ref=win_ref, cw_ref=cw_ref, wout_ref=wout_ref,
                             cvx_ref=cvx_ref)

    @pl.when(i < n_cast)
    def _():
        win_ref[i] = win_f32_ref[...].astype(jnp.bfloat16)

    @pl.when(i < wout_ref.shape[0])
    def _():
        wout_ref[i] = wout_f32_ref[...].astype(jnp.bfloat16)

    @pl.when(i == n_cast)
    def _():
        cvx_ref[0:HIST * batch, :] = jnp.zeros((HIST * batch, D_MODEL), jnp.float32)

    @pl.when(jnp.logical_and(i >= n_cast, i < n_cast + n_tiles))
    def _():
        op_ref[...] = body(xp_ref[...], rows_per_step=batch)

    @pl.when(i == n_cast + n_tiles - 1)
    def _():
        nbufp_ref[...] = cvx_ref[0:HIST * batch, :]

    @pl.when(i == n_cast + n_tiles)
    def _():
        cvx_ref[0:HIST * dec, :] = bufs_ref[...]
        os_ref[...] = body(xs_ref[...], rows_per_step=dec)
        nbufs_ref[...] = cvx_ref[0:HIST * dec, :]


def _conv_mixer(xp, xs, buf_s, gain, w_in, conv_w, w_out, *, layer, mixer, batch, seq):
    dec = xs.shape[0]
    m = ROW_TILE
    n_tiles = (seq * batch) // m
    n_cast = (3 * D_MODEL) // W_CHUNK
    n_out = D_MODEL // W_CHUNK
    kern = functools.partial(_conv_kernel, n_cast=n_cast, n_tiles=n_tiles, batch=batch)
    return pl.pallas_call(
        kern,
        grid=(n_cast + n_tiles + 1,),
        in_specs=[
            _rows_spec(m, n_cast, n_tiles),
            _const_spec((dec, D_MODEL)),
            _const_spec((None, HIST * dec, D_MODEL), (mixer, 0, 0)),
            _const_spec((None, 1, D_MODEL), (layer, 0, 0)),
            _col_chunk_spec(mixer, D_MODEL, n_cast),
            _const_spec((None, CONV_WIDTH, D_MODEL), (mixer, 0, 0)),
            _row_chunk_spec(mixer, D_MODEL, n_out),
        ],
        out_specs=[
            _rows_spec(m, n_cast, n_tiles),
            pl.BlockSpec((dec, D_MODEL), lambda i: (0, 0)),
            pl.BlockSpec((HIST * batch, D_MODEL), lambda i: (0, 0)),
            pl.BlockSpec((HIST * dec, D_MODEL), lambda i: (0, 0)),
        ],
        out_shape=[
            jax.ShapeDtypeStruct((seq * batch, D_MODEL), jnp.float32),
            jax.ShapeDtypeStruct((dec, D_MODEL), jnp.float32),
            jax.ShapeDtypeStruct((HIST * batch, D_MODEL), jnp.float32),
            jax.ShapeDtypeStruct((HIST * dec, D_MODEL), jnp.float32),
        ],
        scratch_shapes=[
            pltpu.VMEM((n_cast, D_MODEL, W_CHUNK), jnp.bfloat16),
            pltpu.VMEM((n_out, W_CHUNK, D_MODEL), jnp.bfloat16),
            pltpu.VMEM((max(HIST * batch + m, (HIST + 1) * dec), D_MODEL), jnp.float32),
        ],
        compiler_params=pltpu.CompilerParams(dimension_semantics=("arbitrary",), vmem_limit_bytes=VMEM_LIMIT),
        name="conv_mixer",
    )(xp, xs, buf_s, gain, w_in, conv_w, w_out)


def _mlp_chunk(ub, up_ref, down_ref, c):
    h = jnp.dot(ub, up_ref[c], preferred_element_type=jnp.float32)
    h = jnp.maximum(h, 0.0)
    h = (h * h).astype(jnp.bfloat16)
    return jnp.dot(h, down_ref[c], preferred_element_type=jnp.float32)


def _mlp_rows(x, g_ref, up_ref, down_ref, gf_ref, *, final_norm):
    ub = _rms_norm(x, g_ref[...]).astype(jnp.bfloat16)
    acc = x
    for c in range(D_FF // W_CHUNK):
        acc = acc + _mlp_chunk(ub, up_ref, down_ref, c)
    if final_norm:
        acc = _rms_norm(acc, gf_ref[...])
    return acc


def _mlp_kernel(xp_ref, xs_ref, g_ref, up_f32_ref, down_f32_ref, gf_ref, op_ref, os_ref,
                up_ref, down_ref, ub0_ref, acc0_ref, *maybe_slab_ref, n_cast, n_tiles, final_norm):
    i = pl.program_id(0)
    body = functools.partial(_mlp_rows, g_ref=g_ref, up_ref=up_ref, down_ref=down_ref, gf_ref=gf_ref,
                             final_norm=final_norm)

    def write_prompt(y):
        if maybe_slab_ref:
            _from_time_major(y, op_ref, maybe_slab_ref[0])
        else:
            op_ref[...] = y

    @pl.when(i == 0)
    def _():
        x = xp_ref[...]
        ub0_ref[...] = _rms_norm(x, g_ref[...]).astype(jnp.bfloat16)
        acc0_ref[...] = x

    @pl.when(i < n_cast)
    def _():
        up_ref[i] = up_f32_ref[...].astype(jnp.bfloat16)
        down_ref[i] = down_f32_ref[...].astype(jnp.bfloat16)
        acc0_ref[...] += _mlp_chunk(ub0_ref[...], up_ref, down_ref, i)

    @pl.when(i == n_cast - 1)
    def _():
        y = acc0_ref[...]
        write_prompt(_rms_norm(y, gf_ref[...]) if final_norm else y)

    @pl.when(jnp.logical_and(i >= n_cast, i < n_cast + n_tiles - 1))
    def _():
        write_prompt(body(xp_ref[...]))

    @pl.when(i == n_cast + n_tiles - 1)
    def _():
        os_ref[...] = body(xs_ref[...])


def _mlp(xp, xs, gain, w_up, w_down, gain_final, *, layer, batch, seq, final_norm):
    dec = xs.shape[0]
    m = ROW_TILE
    steps = m // batch
    n_tiles = (seq * batch) // m
    n_cast = D_FF // W_CHUNK
    scratch = [
        pltpu.VMEM((n_cast, D_MODEL, W_CHUNK), jnp.bfloat16),
        pltpu.VMEM((n_cast, W_CHUNK, D_MODEL), jnp.bfloat16),
        pltpu.VMEM((m, D_MODEL), jnp.bfloat16),
        pltpu.VMEM((m, D_MODEL), jnp.float32),
    ]
    first = n_cast - 1
    if final_norm:
        scratch.append(pltpu.VMEM((D_MODEL // LANES, m, LANES), jnp.float32))
        op_spec = _natural_spec(batch, steps, first, n_tiles)
        op_shape = jax.ShapeDtypeStruct((batch, seq, D_MODEL), jnp.float32)
    else:
        op_spec = _rows_spec(m, first, n_tiles)
        op_shape = jax.ShapeDtypeStruct((seq * batch, D_MODEL), jnp.float32)
    kern = functools.partial(_mlp_kernel, n_cast=n_cast, n_tiles=n_tiles, final_norm=final_norm)
    return pl.pallas_call(
        kern,
        grid=(n_cast + n_tiles,),
        in_specs=[
            _rows_spec(m, first, n_tiles),
            _const_spec((dec, D_MODEL)),
            _const_spec((None, 1, D_MODEL), (layer, 0, 0)),
            _col_chunk_spec(layer, D_MODEL, n_cast),
            _row_chunk_spec(layer, D_MODEL, n_cast),
            _const_spec((1, D_MODEL)),
        ],
        out_specs=[op_spec, pl.BlockSpec((dec, D_MODEL), lambda i: (0, 0))],
        out_shape=[op_shape, jax.ShapeDtypeStruct((dec, D_MODEL), jnp.float32)],
        scratch_shapes=scratch,
        compiler_params=pltpu.CompilerParams(dimension_semantics=("arbitrary",), vmem_limit_bytes=VMEM_LIMIT),
        name="mlp",
    )(xp, xs, gain, w_up, w_down, gain_final)


def _glu_mlp_kernel(xp_ref, y_ref, xs_ref, gmix_ref, d_ref, wglu_f32_ref, g_ref, up_f32_ref, down_f32_ref,
                    op_ref, os_ref, wglu_ref, up_ref, down_ref, ub0_ref, acc0_ref, *maybe_slab_ref,
                    n_pre, n_cast, n_tiles):
    i = pl.program_id(0)
    body = functools.partial(_mlp_rows, g_ref=g_ref, up_ref=up_ref, down_ref=down_ref, gf_ref=None,
                             final_norm=False)

    def mixed():
        x = _to_time_major(xp_ref, maybe_slab_ref[0]) if maybe_slab_ref else xp_ref[...]
        return _s5_finish_rows(x, y_ref, gmix_ref, d_ref, wglu_ref)

    @pl.when(i < n_pre)
    def _():
        wglu_ref[i] = wglu_f32_ref[...].astype(jnp.bfloat16)

    @pl.when(i == n_pre)
    def _():
        x = mixed()
        ub0_ref[...] = _rms_norm(x, g_ref[...]).astype(jnp.bfloat16)
        acc0_ref[...] = x

    @pl.when(jnp.logical_and(i >= n_pre, i < n_pre + n_cast))
    def _():
        c = i - n_pre
        up_ref[c] = up_f32_ref[...].astype(jnp.bfloat16)
        down_ref[c] = down_f32_ref[...].astype(jnp.bfloat16)
        acc0_ref[...] += _mlp_chunk(ub0_ref[...], up_ref, down_ref, c)

    @pl.when(i == n_pre + n_cast - 1)
    def _():
        op_ref[...] = acc0_ref[...]

    @pl.when(jnp.logical_and(i >= n_pre + n_cast, i < n_pre + n_cast + n_tiles - 1))
    def _():
        op_ref[...] = body(mixed())

    @pl.when(i == n_pre + n_cast + n_tiles - 1)
    def _():
        os_ref[...] = body(xs_ref[...])


def _glu_mlp(xp, y, xs, gain_mix, d_skip, w_glu, gain, w_up, w_down, *, layer, mixer, batch, seq):
    natural = xp.ndim == 3
    dec = xs.shape[0]
    m = ROW_TILE
    n_tiles = (seq * batch) // m
    n_pre = (2 * D_MODEL) // W_CHUNK
    n_cast = D_FF // W_CHUNK
    first = n_pre + n_cast - 1
    xp_spec = (_natural_spec(batch, m // batch, first, n_tiles) if natural else _rows_spec(m, first, n_tiles))
    slab = [pltpu.VMEM((D_MODEL // LANES, m, LANES), jnp.float32)] if natural else []
    return pl.pallas_call(
        functools.partial(_glu_mlp_kernel, n_pre=n_pre, n_cast=n_cast, n_tiles=n_tiles),
        grid=(n_pre + n_cast + n_tiles,),
        in_specs=[
            xp_spec,
            _rows_spec(m, first, n_tiles),
            _const_spec((dec, D_MODEL)),
            _const_spec((None, 1, D_MODEL), (layer, 0, 0)),
            _const_spec((None, 1, D_MODEL), (mixer, 0, 0)),
            _col_chunk_spec(mixer, D_MODEL, n_pre),
            _const_spec((None, 1, D_MODEL), (layer, 0, 0)),
            pl.BlockSpec((None, D_MODEL, W_CHUNK), lambda i: (layer, 0, jnp.clip(i - n_pre, 0, n_cast - 1))),
            pl.BlockSpec((None, W_CHUNK, D_MODEL), lambda i: (layer, jnp.clip(i - n_pre, 0, n_cast - 1), 0)),
        ],
        out_specs=[_rows_spec(m, first, n_tiles), pl.BlockSpec((dec, D_MODEL), lambda i: (0, 0))],
        out_shape=[jax.ShapeDtypeStruct((seq * batch, D_MODEL), jnp.float32),
                   jax.ShapeDtypeStruct((dec, D_MODEL), jnp.float32)],
        scratch_shapes=[
            pltpu.VMEM((n_pre, D_MODEL, W_CHUNK), jnp.bfloat16),
            pltpu.VMEM((n_cast, D_MODEL, W_CHUNK), jnp.bfloat16),
            pltpu.VMEM((n_cast, W_CHUNK, D_MODEL), jnp.bfloat16),
            pltpu.VMEM((m, D_MODEL), jnp.bfloat16),
            pltpu.VMEM((m, D_MODEL), jnp.float32),
        ] + slab,
        compiler_params=pltpu.CompilerParams(dimension_semantics=("arbitrary",), vmem_limit_bytes=VMEM_LIMIT),
        name="glu_mlp",
    )(xp, y, xs, gain_mix, d_skip, w_glu, gain, w_up, w_down)


def _s5_discretise(a_re, a_im, log_dt, b_re, b_im):
    dt = jnp.exp(log_dt)[:, None]
    k = jnp.arange(CHUNK + 1, dtype=jnp.float32)[:, None, None]
    mag = jnp.exp(k * (a_re * dt))
    pw_re = mag * jnp.cos(k * (a_im * dt))
    pw_im = mag * jnp.sin(k * (a_im * dt))
    ab_re, ab_im = pw_re[1], pw_im[1]
    den = a_re * a_re + a_im * a_im
    q_re = ((ab_re - 1.0) * a_re + ab_im * a_im) / den
    q_im = (ab_im * a_re - (ab_re - 1.0) * a_im) / den
    bb_re = q_re[..., None] * b_re - q_im[..., None] * b_im
    bb_im = q_re[..., None] * b_im + q_im[..., None] * b_re
    return pw_re, pw_im, bb_re, bb_im


def _s5_step_params(pw_re, pw_im, bb_re, bb_im, c_re, c_im):
    gps = MXU_K // GROUP_SIZE
    eye = jnp.eye(gps, dtype=jnp.float32)

    def pack_b(w):
        w = w.reshape(N_SLABS, gps, STATE_DIM, GROUP_SIZE)
        return jnp.einsum('sgph,gk->sghkp', w, eye).reshape(N_SLABS, MXU_K, SLAB_STATE)

    def pack_c(w):
        w = w.reshape(N_SLABS, gps, GROUP_SIZE, STATE_DIM)
        return jnp.einsum('sghp,gk->sgpkh', w, eye).reshape(N_SLABS, SLAB_STATE, MXU_K)

    wb = jnp.concatenate([pack_b(bb_re), pack_b(bb_im)], axis=2).astype(jnp.bfloat16)
    wc = jnp.concatenate([pack_c(c_re), pack_c(-c_im)], axis=1).astype(jnp.bfloat16)
    return pw_re[1].reshape(1, N_STATE), pw_im[1].reshape(1, N_STATE), wb, wc


def _s5_chunk_params(pw_re, pw_im, bb_re, bb_im, c_re, c_im):
    ps_re = jnp.transpose(pw_re[CHUNK - 1::-1], (1, 0, 2))[:, :, None, :]
    ps_im = jnp.transpose(pw_im[CHUNK - 1::-1], (1, 0, 2))[:, :, None, :]
    bt_re = jnp.transpose(bb_re, (0, 2, 1))[:, None]
    bt_im = jnp.transpose(bb_im, (0, 2, 1))[:, None]
    ws = jnp.concatenate([ps_re * bt_re - ps_im * bt_im, ps_re * bt_im + ps_im * bt_re],
                         axis=-1).reshape(N_GROUPS, MXU_K, 2 * STATE_DIM)
    pt_re = jnp.transpose(pw_re, (1, 2, 0))
    pt_im = jnp.transpose(pw_im, (1, 2, 0))
    ct_re = jnp.transpose(c_re, (0, 2, 1))[:, :, None, :]
    ct_im = jnp.transpose(c_im, (0, 2, 1))[:, :, None, :]

    def ca(t0):
        p_r = pt_re[:, :, t0:t0 + CHUNK, None]
        p_i = pt_im[:, :, t0:t0 + CHUNK, None]
        return ((ct_re * p_r - ct_im * p_i).reshape(N_GROUPS, STATE_DIM, MXU_K),
                (ct_re * p_i + ct_im * p_r).reshape(N_GROUPS, STATE_DIM, MXU_K))

    ca0_re, ca0_im = ca(0)
    ca1_re, ca1_im = ca(1)
    hi = lax.Precision.HIGHEST
    kc = (jnp.einsum('gph,gpx->ghx', bb_re, ca0_re, precision=hi)
          - jnp.einsum('gph,gpx->ghx', bb_im, ca0_im, precision=hi))
    wy = jnp.concatenate([ca1_re, -ca1_im], axis=1)
    a1 = jnp.concatenate([pw_re[CHUNK], pw_re[CHUNK]], axis=-1)
    a2 = jnp.concatenate([-pw_im[CHUNK], pw_im[CHUNK]], axis=-1)
    return ws.astype(jnp.bfloat16), kc, wy.astype(jnp.bfloat16), a1, a2


def kernel(x_prompt, x_sample, state_ssm_re, state_ssm_im, state_conv, norm_mix, norm_mlp, norm_final,
           ssm_a_re, ssm_a_im, ssm_log_dt, ssm_b_re, ssm_b_im, ssm_c_re, ssm_c_im, ssm_d, ssm_w_glu,
           conv_w_in, conv_w, conv_w_out, mlp_w_up, mlp_w_down):
    batch, seq, _ = x_prompt.shape
    dec = x_sample.shape[0]
    n_ssm = ssm_a_re.shape[0]
    n_conv = conv_w_in.shape[0]

    g_mix = norm_mix.reshape(DEPTH, 1, D_MODEL)
    g_mlp = norm_mlp.reshape(DEPTH, 1, D_MODEL)
    g_final = norm_final.reshape(1, D_MODEL)
    d_skip = ssm_d.reshape(n_ssm, 1, D_MODEL)
    step_params, chunk_params = [], []
    for j in range(n_ssm):
        disc = _s5_discretise(ssm_a_re[j], ssm_a_im[j], ssm_log_dt[j], ssm_b_re[j], ssm_b_im[j])
        step_params.append(_s5_step_params(*disc, ssm_c_re[j], ssm_c_im[j]))
        chunk_params.append(_s5_chunk_params(*disc, ssm_c_re[j], ssm_c_im[j]))
    h0_re = state_ssm_re.reshape(n_ssm, dec, N_STATE)
    h0_im = state_ssm_im.reshape(n_ssm, dec, N_STATE)
    buf_s = jnp.transpose(state_conv, (0, 2, 1, 3)).reshape(n_conv, HIST * dec, D_MODEL)

    xp = x_prompt
    xs = x_sample.reshape(dec, D_MODEL)
    pre, pim, sre, sim, pbuf, sbuf = [], [], [], [], [], []
    for i in range(DEPTH):
        j = i // 2
        if i % 2 == 0:
            y, a, b = _s5_core(xp, g_mix, chunk_params[j], layer=i, batch=batch, seq=seq)
            xs, c, d = _s5_sample(xs, h0_re, h0_im, g_mix, step_params[j], d_skip, ssm_w_glu, layer=i, mixer=j)
            pre.append(a), pim.append(b), sre.append(c), sim.append(d)
            xp, xs = _glu_mlp(xp, y, xs, g_mix, d_skip, ssm_w_glu, g_mlp, mlp_w_up, mlp_w_down,
                              layer=i, mixer=j, batch=batch, seq=seq)
        else:
            xp, xs, a, b = _conv_mixer(xp, xs, buf_s, g_mix, conv_w_in, conv_w, conv_w_out,
                                       layer=i, mixer=j, batch=batch, seq=seq)
            pbuf.append(a), sbuf.append(b)
            xp, xs = _mlp(xp, xs, g_mlp, mlp_w_up, mlp_w_down, g_final, layer=i, batch=batch, seq=seq,
                          final_norm=(i == DEPTH - 1))

    def states(parts, n):
        return jnp.stack(parts).reshape(n_ssm, n, N_GROUPS, STATE_DIM)

    def bufs(parts, n):
        return jnp.transpose(jnp.stack(parts).reshape(n_conv, HIST, n, D_MODEL), (0, 2, 1, 3))

    return (xp, xs.reshape(dec, 1, D_MODEL), states(pre, batch), states(pim, batch), bufs(pbuf, batch),
            states(sre, dec), states(sim, dec), bufs(sbuf, dec))
```

```python
import functools

import jax
import jax.numpy as jnp
from jax import lax
from jax.experimental import pallas as pl
from jax.experimental.pallas import tpu as pltpu

D_MODEL = 1024
DEPTH = 4
GROUP_SIZE = 16
N_GROUPS = D_MODEL // GROUP_SIZE
STATE_DIM = 64
N_STATE = N_GROUPS * STATE_DIM
CONV_WIDTH = 3
HIST = CONV_WIDTH - 1
D_FF = 4 * D_MODEL
RMS_EPS = 1e-6

LANES = 128
MXU_K = 256
N_SLABS = D_MODEL // MXU_K
SLAB_STATE = N_STATE // N_SLABS
CHUNK = MXU_K // GROUP_SIZE
W_CHUNK = 512
VMEM_LIMIT = 58 * 1024 * 1024
ROW_TILE = 512
S5_TILE = 1024
GLU_ROWS = 256
PIPE_COLS = 2


def _rms_norm(x, g):
    ms = jnp.mean(x * x, axis=-1, keepdims=True)
    return x * lax.rsqrt(ms + RMS_EPS) * g


def _const_spec(shape, index=None):
    index = (0,) * len(shape) if index is None else index
    return pl.BlockSpec(shape, lambda i: index, pipeline_mode=pl.Buffered(1))


def _rows_spec(m, first_step, n_tiles):
    return pl.BlockSpec((m, D_MODEL), lambda i: (jnp.clip(i - first_step, 0, n_tiles - 1), 0))


def _natural_spec(batch, steps, first_step, n_tiles):
    return pl.BlockSpec((batch, steps, D_MODEL),
                        lambda i: (0, jnp.clip(i - first_step, 0, n_tiles - 1), 0))


def _col_chunk_spec(layer, n_rows, n_chunks):
    return pl.BlockSpec((None, n_rows, W_CHUNK), lambda i: (layer, 0, jnp.minimum(i, n_chunks - 1)))


def _row_chunk_spec(layer, n_cols, n_chunks):
    return pl.BlockSpec((None, W_CHUNK, n_cols), lambda i: (layer, jnp.minimum(i, n_chunks - 1), 0))


def _to_time_major(x_ref, slab_ref):
    batch, steps, _ = x_ref.shape
    for c in range(D_MODEL // LANES):
        for b in range(batch):
            slab_ref[c, pl.ds(b, steps, stride=batch), :] = x_ref[b, :, c * LANES:(c + 1) * LANES]
    return jnp.concatenate([slab_ref[c] for c in range(D_MODEL // LANES)], axis=1)


def _from_time_major(y, o_ref, slab_ref):
    batch, steps, _ = o_ref.shape
    for c in range(D_MODEL // LANES):
        slab_ref[c] = y[:, c * LANES:(c + 1) * LANES]
        for b in range(batch):
            o_ref[b, :, c * LANES:(c + 1) * LANES] = slab_ref[c, pl.ds(b, steps, stride=batch), :]


def _glu_tail(x, yb, wglu_ref):
    n_half = D_MODEL // W_CHUNK
    outs = []
    for c in range(n_half):
        z_lin = jnp.dot(yb, wglu_ref[c], preferred_element_type=jnp.float32)
        z_gate = jnp.dot(yb, wglu_ref[n_half + c], preferred_element_type=jnp.float32)
        outs.append(z_lin * jax.nn.sigmoid(z_gate))
    return x + jnp.concatenate(outs, axis=1)


def _s5_step_rows(x, hre_ref, him_ref, g_ref, are_ref, aim_ref, wb_ref, wc_ref, d_ref, wglu_ref,
                  bu_ref, hb_ref, y_ref):
    u = _rms_norm(x, g_ref[...])
    ub = u.astype(jnp.bfloat16)
    for s in range(N_SLABS):
        ch = slice(s * MXU_K, (s + 1) * MXU_K)
        bu_ref[...] = jnp.dot(ub[:, ch], wb_ref[s], preferred_element_type=jnp.float32)
        for l0 in range(0, SLAB_STATE, LANES):
            re_l = slice(l0, l0 + LANES)
            im_l = slice(SLAB_STATE + l0, SLAB_STATE + l0 + LANES)
            st_l = slice(s * SLAB_STATE + l0, s * SLAB_STATE + l0 + LANES)
            a_r, a_i = are_ref[:, st_l], aim_ref[:, st_l]
            h_r, h_i = hre_ref[:, st_l], him_ref[:, st_l]
            n_r = a_r * h_r - a_i * h_i + bu_ref[:, re_l]
            n_i = a_r * h_i + a_i * h_r + bu_ref[:, im_l]
            hre_ref[:, st_l] = n_r
            him_ref[:, st_l] = n_i
            hb_ref[:, re_l] = n_r.astype(jnp.bfloat16)
            hb_ref[:, im_l] = n_i.astype(jnp.bfloat16)
        ys = jnp.dot(hb_ref[...], wc_ref[s], preferred_element_type=jnp.float32)
        ys = ys + d_ref[:, ch] * u[:, ch]
        y_ref[:, ch] = jax.nn.gelu(ys).astype(jnp.bfloat16)
    return _glu_tail(x, y_ref[...], wglu_ref)


def _s5_sample_kernel(xs_ref, h0re_ref, h0im_ref, g_ref, are_ref, aim_ref, wb_ref, wc_ref, d_ref, wglu_f32_ref,
                      os_ref, sre_ref, sim_ref, wglu_ref, bu_ref, hb_ref, y_ref, *, n_cast):
    i = pl.program_id(0)

    @pl.when(i < n_cast)
    def _():
        wglu_ref[i] = wglu_f32_ref[...].astype(jnp.bfloat16)

    @pl.when(i == n_cast)
    def _():
        sre_ref[...] = h0re_ref[...]
        sim_ref[...] = h0im_ref[...]
        os_ref[...] = _s5_step_rows(xs_ref[...], sre_ref, sim_ref, g_ref, are_ref, aim_ref, wb_ref, wc_ref,
                                    d_ref, wglu_ref, bu_ref, hb_ref, y_ref)


def _s5_sample(xs, h0_re, h0_im, gain, step_params, d_skip, w_glu, *, layer, mixer):
    a_re, a_im, wb, wc = step_params
    dec = xs.shape[0]
    n_cast = (2 * D_MODEL) // W_CHUNK
    return pl.pallas_call(
        functools.partial(_s5_sample_kernel, n_cast=n_cast),
        grid=(n_cast + 1,),
        in_specs=[
            _const_spec((dec, D_MODEL)),
            _const_spec((None, dec, N_STATE), (mixer, 0, 0)), _const_spec((None, dec, N_STATE), (mixer, 0, 0)),
            _const_spec((None, 1, D_MODEL), (layer, 0, 0)),
            _const_spec((1, N_STATE)), _const_spec((1, N_STATE)),
            _const_spec((N_SLABS, MXU_K, 2 * SLAB_STATE)),
            _const_spec((N_SLABS, 2 * SLAB_STATE, MXU_K)),
            _const_spec((None, 1, D_MODEL), (mixer, 0, 0)),
            _col_chunk_spec(mixer, D_MODEL, n_cast),
        ],
        out_specs=[
            pl.BlockSpec((dec, D_MODEL), lambda i: (0, 0)),
            pl.BlockSpec((dec, N_STATE), lambda i: (0, 0)), pl.BlockSpec((dec, N_STATE), lambda i: (0, 0)),
        ],
        out_shape=[
            jax.ShapeDtypeStruct((dec, D_MODEL), jnp.float32),
            jax.ShapeDtypeStruct((dec, N_STATE), jnp.float32), jax.ShapeDtypeStruct((dec, N_STATE), jnp.float32),
        ],
        scratch_shapes=[
            pltpu.VMEM((n_cast, D_MODEL, W_CHUNK), jnp.bfloat16),
            pltpu.VMEM((dec, 2 * SLAB_STATE), jnp.float32),
            pltpu.VMEM((dec, 2 * SLAB_STATE), jnp.bfloat16),
            pltpu.VMEM((dec, D_MODEL), jnp.bfloat16),
        ],
        compiler_params=pltpu.CompilerParams(dimension_semantics=("arbitrary",), vmem_limit_bytes=VMEM_LIMIT),
        name="s5_sample",
    )(xs, h0_re, h0_im, gain, a_re, a_im, wb, wc, d_skip, w_glu)


def _block_transpose(sets, lane_block):
    sets = [list(vs) for vs in sets]
    n = len(sets[0])
    d = 1
    while d < n:
        upper = (lane_block & d) != 0
        for vs in sets:
            for i in range(n):
                if i & d == 0:
                    a, b = vs[i], vs[i + d]
                    vs[i] = jnp.where(upper, pltpu.roll(b, GROUP_SIZE * d, 1), a)
                    vs[i + d] = jnp.where(upper, b, pltpu.roll(a, LANES - GROUP_SIZE * d, 1))
        d *= 2
    return sets


def _s5_chunk_rows(x, hst_ref, g_ref, ws_ref, wt_ref, wy_ref, a1_ref, a2_ref, y_ref,
                   lhs_ref, sc_ref, hp_ref, yg_ref, *, batch):
    m = x.shape[0]
    n_chunks = m // (CHUNK * batch)
    blocks = LANES // GROUP_SIZE
    n_cols = D_MODEL // LANES
    n_halves = CHUNK // blocks
    pass_chunks = min(n_chunks, 4)
    pass_rows = pass_chunks * batch
    u = _rms_norm(x, g_ref[...])
    lane_block = lax.broadcasted_iota(jnp.int32, (pass_rows, LANES), 1) // GROUP_SIZE

    def step_rows(c, s):
        r0 = (c * CHUNK + s) * batch
        return slice(r0, r0 + batch)

    def gather(qs):
        for c0 in range(0, n_chunks, pass_chunks):
            prow = slice(c0 * batch, c0 * batch + pass_rows)
            sets = [[jnp.concatenate([u[step_rows(c0 + c, k * blocks + s), q * LANES:(q + 1) * LANES]
                                      for c in range(pass_chunks)], axis=0).astype(jnp.bfloat16)
                     for s in range(blocks)]
                    for q in qs for k in range(n_halves)]
            w = _block_transpose(sets, lane_block)
            for iq, q in enumerate(qs):
                for j in range(blocks):
                    lhs = jnp.concatenate([w[iq * n_halves + k][j] for k in range(n_halves)], axis=1)
                    lhs_ref[q * blocks + j, prow, :] = lhs

    def project(gs):
        for g in gs:
            sc_ref[g] = jnp.dot(lhs_ref[g], ws_ref[g], preferred_element_type=jnp.float32)
        for g in gs:
            recur(g)
        for g in gs:
            yg_ref[g] = (jnp.dot(lhs_ref[g], wt_ref[g], preferred_element_type=jnp.float32)
                         + jnp.dot(hp_ref[g], wy_ref[g], preferred_element_type=jnp.float32)
                         ).astype(jnp.bfloat16)

    def recur(g):
        a1 = a1_ref[g:g + 1, :]
        a2 = a2_ref[g:g + 1, :]
        sc = sc_ref[g]
        sc_sw = pltpu.roll(sc, STATE_DIM, 1)
        h = hst_ref[g]
        hs = pltpu.roll(h, STATE_DIM, 1)
        h_prev = []
        for c in range(n_chunks):
            h_prev.append(h)
            rows_c = slice(c * batch, (c + 1) * batch)
            h, hs = a1 * h + a2 * hs + sc[rows_c], a1 * hs - a2 * h + sc_sw[rows_c]
        hst_ref[g] = h
        hp_ref[g] = jnp.concatenate(h_prev, axis=0).astype(jnp.bfloat16)

    def scatter(qs):
        for c0 in range(0, n_chunks, pass_chunks):
            prow = slice(c0 * batch, c0 * batch + pass_rows)
            sets = [[yg_ref[q * blocks + j, prow, k * LANES:(k + 1) * LANES] for j in range(blocks)]
                    for q in qs for k in range(n_halves)]
            z = _block_transpose(sets, lane_block)
            for iq, q in enumerate(qs):
                for k in range(n_halves):
                    for s in range(blocks):
                        zs = z[iq * n_halves + k][s].astype(jnp.float32)
                        for c in range(pass_chunks):
                            y_ref[step_rows(c0 + c, k * blocks + s), q * LANES:(q + 1) * LANES] = (
                                zs[c * batch:(c + 1) * batch])

    col_blocks = [list(range(q0, q0 + PIPE_COLS)) for q0 in range(0, n_cols, PIPE_COLS)]
    grp_blocks = [[q * blocks + j for q in qs for j in range(blocks)] for qs in col_blocks]
    n_blk = len(col_blocks)
    for step in range(n_blk + 2):
        if step < n_blk:
            gather(col_blocks[step])
        if 1 <= step <= n_blk:
            project(grp_blocks[step - 1])
        if step >= 2:
            scatter(col_blocks[step - 2])


def _s5_core_kernel(xp_ref, g_ref, ws_ref, kc_ref, wy_ref, a1_ref, a2_ref,
                    y_ref, pre_ref, pim_ref, wt_ref, hst_ref, lhs_ref, sc_ref, hp_ref, yg_ref,
                    *maybe_slab_ref, n_tiles, batch):
    i = pl.program_id(0)

    @pl.when(i == 0)
    def _():
        hst_ref[...] = jnp.zeros_like(hst_ref)
        lane = lax.broadcasted_iota(jnp.int32, (GROUP_SIZE, MXU_K), 1)

        def build(g, carry):
            kc = kc_ref[g]
            for s in range(CHUNK):
                blk = kc if s == 0 else jnp.where(lane >= s * GROUP_SIZE, pltpu.roll(kc, s * GROUP_SIZE, 1), 0.0)
                wt_ref[g, s * GROUP_SIZE:(s + 1) * GROUP_SIZE, :] = blk.astype(jnp.bfloat16)
            return carry

        lax.fori_loop(0, N_GROUPS, build, 0)

    x = _to_time_major(xp_ref, maybe_slab_ref[0]) if maybe_slab_ref else xp_ref[...]
    _s5_chunk_rows(x, hst_ref, g_ref, ws_ref, wt_ref, wy_ref, a1_ref, a2_ref,
                   y_ref, lhs_ref, sc_ref, hp_ref, yg_ref, batch=batch)

    @pl.when(i == n_tiles - 1)
    def _():
        low = lax.broadcasted_iota(jnp.int32, (batch, LANES), 1) < STATE_DIM
        for g in range(0, N_GROUPS, 2):
            h0, h1 = hst_ref[g], hst_ref[g + 1]
            cols = slice(g * STATE_DIM, (g + 2) * STATE_DIM)
            pre_ref[:, cols] = jnp.where(low, h0, pltpu.roll(h1, STATE_DIM, 1))
            pim_ref[:, cols] = jnp.where(low, pltpu.roll(h0, STATE_DIM, 1), h1)


def _s5_core(xp, gain, chunk_params, *, layer, batch, seq):
    natural = xp.ndim == 3
    ws, kc, wy, a1, a2 = chunk_params
    m = S5_TILE
    n_tiles = (seq * batch) // m

    def xp_spec(first_step):
        return (_natural_spec(batch, m // batch, first_step, n_tiles) if natural
                else _rows_spec(m, first_step, n_tiles))

    slab = [pltpu.VMEM((D_MODEL // LANES, m, LANES), jnp.float32)] if natural else []
    scratch = [
        pltpu.VMEM((N_GROUPS, MXU_K, MXU_K), jnp.bfloat16),
        pltpu.VMEM((N_GROUPS, batch, 2 * STATE_DIM), jnp.float32),
        pltpu.VMEM((N_GROUPS, m // CHUNK, MXU_K), jnp.bfloat16),
        pltpu.VMEM((N_GROUPS, m // CHUNK, 2 * STATE_DIM), jnp.float32),
        pltpu.VMEM((N_GROUPS, m // CHUNK, 2 * STATE_DIM), jnp.bfloat16),
        pltpu.VMEM((N_GROUPS, m // CHUNK, MXU_K), jnp.bfloat16),
    ]
    y, pre, pim = pl.pallas_call(
        functools.partial(_s5_core_kernel, n_tiles=n_tiles, batch=batch),
        grid=(n_tiles,),
        in_specs=[
            xp_spec(0),
            _const_spec((None, 1, D_MODEL), (layer, 0, 0)),
            _const_spec((N_GROUPS, MXU_K, 2 * STATE_DIM)),
            _const_spec((N_GROUPS, GROUP_SIZE, MXU_K)),
            _const_spec((N_GROUPS, 2 * STATE_DIM, MXU_K)),
            _const_spec((N_GROUPS, 2 * STATE_DIM)), _const_spec((N_GROUPS, 2 * STATE_DIM)),
        ],
        out_specs=[
            _rows_spec(m, 0, n_tiles),
            pl.BlockSpec((batch, N_STATE), lambda i: (0, 0)), pl.BlockSpec((batch, N_STATE), lambda i: (0, 0)),
        ],
        out_shape=[
            jax.ShapeDtypeStruct((seq * batch, D_MODEL), jnp.float32),
            jax.ShapeDtypeStruct((batch, N_STATE), jnp.float32), jax.ShapeDtypeStruct((batch, N_STATE), jnp.float32),
        ],
        scratch_shapes=scratch + slab,
        compiler_params=pltpu.CompilerParams(dimension_semantics=("arbitrary",), vmem_limit_bytes=VMEM_LIMIT),
        name="s5_core",
    )(xp, gain, ws, kc, wy, a1, a2)
    return y, pre, pim


def _s5_finish_rows(x, y_ref, g_ref, d_ref, wglu_ref):
    parts = []
    for r0 in range(0, x.shape[0], GLU_ROWS):
        rows = slice(r0, r0 + GLU_ROWS)
        ys = y_ref[rows, :] + d_ref[...] * _rms_norm(x[rows], g_ref[...])
        parts.append(_glu_tail(x[rows], jax.nn.gelu(ys).astype(jnp.bfloat16), wglu_ref))
    return jnp.concatenate(parts, axis=0)


def _conv_rows(x, g_ref, win_ref, cw_ref, wout_ref, cvx_ref, *, rows_per_step):
    br = rows_per_step
    m = x.shape[0]
    hist = HIST * br
    n_d = D_MODEL // W_CHUNK
    ub = _rms_norm(x, g_ref[...]).astype(jnp.bfloat16)
    bcv = [jnp.dot(ub, win_ref[c], preferred_element_type=jnp.float32) for c in range(3 * n_d)]
    outs = 0.0
    for c in range(n_d):
        cols = slice(c * W_CHUNK, (c + 1) * W_CHUNK)
        cv = bcv[n_d + c] * bcv[2 * n_d + c]
        cvx_ref[hist:hist + m, cols] = cv
        y = cw_ref[CONV_WIDTH - 1:CONV_WIDTH, cols] * cv
        for k in range(CONV_WIDTH - 1):
            y = y + cw_ref[k:k + 1, cols] * cvx_ref[k * br:k * br + m, cols]
        gated = (bcv[c] * y).astype(jnp.bfloat16)
        outs = outs + jnp.dot(gated, wout_ref[c], preferred_element_type=jnp.float32)
    cvx_ref[0:hist, :] = cvx_ref[m:m + hist, :]
    return x + outs


def _conv_kernel(xp_ref, xs_ref, bufs_ref, g_ref, win_f32_ref, cw_ref, wout_f32_ref,
                 op_ref, os_ref, nbufp_ref, nbufs_ref, win_ref, wout_ref, cvx_ref, *, n_cast, n_tiles, batch):
    i = pl.program_id(0)
    dec = xs_ref.shape[0]
    body = functools.partial(_conv_rows, g_ref=g_ref, win_ref=win_ref, cw_ref=cw_ref, wout_ref=wout_ref,
                             cvx_ref=cvx_ref)

    @pl.when(i < n_cast)
    def _():
        win_ref[i] = win_f32_ref[...].astype(jnp.bfloat16)

    @pl.when(i < wout_ref.shape[0])
    def _():
        wout_ref[i] = wout_f32_ref[...].astype(jnp.bfloat16)

    @pl.when(i == n_cast)
    def _():
        cvx_ref[0:HIST * batch, :] = jnp.zeros((HIST * batch, D_MODEL), jnp.float32)

    @pl.when(jnp.logical_and(i >= n_cast, i < n_cast + n_tiles))
    def _():
        op_ref[...] = body(xp_ref[...], rows_per_step=batch)

    @pl.when(i == n_cast + n_tiles - 1)
    def _():
        nbufp_ref[...] = cvx_ref[0:HIST * batch, :]

    @pl.when(i == n_cast + n_tiles)
    def _():
        cvx_ref[0:HIST * dec, :] = bufs_ref[...]
        os_ref[...] = body(xs_ref[...], rows_per_step=dec)
        nbufs_ref[...] = cvx_ref[0:HIST * dec, :]


def _conv_mixer(xp, xs, buf_s, gain, w_in, conv_w, w_out, *, layer, mixer, batch, seq):
    dec = xs.shape[0]
    m = ROW_TILE
    n_tiles = (seq * batch) // m
    n_cast = (3 * D_MODEL) // W_CHUNK
    n_out = D_MODEL // W_CHUNK
    kern = functools.partial(_conv_kernel, n_cast=n_cast, n_tiles=n_tiles, batch=batch)
    return pl.pallas_call(
        kern,
        grid=(n_cast + n_tiles + 1,),
        in_specs=[
            _rows_spec(m, n_cast, n_tiles),
            _const_spec((dec, D_MODEL)),
            _const_spec((None, HIST * dec, D_MODEL), (mixer, 0, 0)),
            _const_spec((None, 1, D_MODEL), (layer, 0, 0)),
            _col_chunk_spec(mixer, D_MODEL, n_cast),
            _const_spec((None, CONV_WIDTH, D_MODEL), (mixer, 0, 0)),
            _row_chunk_spec(mixer, D_MODEL, n_out),
        ],
        out_specs=[
            _rows_spec(m, n_cast, n_tiles),
            pl.BlockSpec((dec, D_MODEL), lambda i: (0, 0)),
            pl.BlockSpec((HIST * batch, D_MODEL), lambda i: (0, 0)),
            pl.BlockSpec((HIST * dec, D_MODEL), lambda i: (0, 0)),
        ],
        out_shape=[
            jax.ShapeDtypeStruct((seq * batch, D_MODEL), jnp.float32),
            jax.ShapeDtypeStruct((dec, D_MODEL), jnp.float32),
            jax.ShapeDtypeStruct((HIST * batch, D_MODEL), jnp.float32),
            jax.ShapeDtypeStruct((HIST * dec, D_MODEL), jnp.float32),
        ],
        scratch_shapes=[
            pltpu.VMEM((n_cast, D_MODEL, W_CHUNK), jnp.bfloat16),
            pltpu.VMEM((n_out, W_CHUNK, D_MODEL), jnp.bfloat16),
            pltpu.VMEM((max(HIST * batch + m, (HIST + 1) * dec), D_MODEL), jnp.float32),
        ],
        compiler_params=pltpu.CompilerParams(dimension_semantics=("arbitrary",), vmem_limit_bytes=VMEM_LIMIT),
        name="conv_mixer",
    )(xp, xs, buf_s, gain, w_in, conv_w, w_out)


def _mlp_chunk(ub, up_ref, down_ref, c):
    h = jnp.dot(ub, up_ref[c], preferred_element_type=jnp.float32)
    h = jnp.maximum(h, 0.0)
    h = (h * h).astype(jnp.bfloat16)
    return jnp.dot(h, down_ref[c], preferred_element_type=jnp.float32)


def _mlp_rows(x, g_ref, up_ref, down_ref, gf_ref, *, final_norm):
    ub = _rms_norm(x, g_ref[...]).astype(jnp.bfloat16)
    acc = x
    for c in range(D_FF // W_CHUNK):
        acc = acc + _mlp_chunk(ub, up_ref, down_ref, c)
    if final_norm:
        acc = _rms_norm(acc, gf_ref[...])
    return acc


def _mlp_kernel(xp_ref, xs_ref, g_ref, up_f32_ref, down_f32_ref, gf_ref, op_ref, os_ref,
                up_ref, down_ref, ub0_ref, acc0_ref, *maybe_slab_ref, n_cast, n_tiles, final_norm):
    i = pl.program_id(0)
    body = functools.partial(_mlp_rows, g_ref=g_ref, up_ref=up_ref, down_ref=down_ref, gf_ref=gf_ref,
                             final_norm=final_norm)

    def write_prompt(y):
        if maybe_slab_ref:
            _from_time_major(y, op_ref, maybe_slab_ref[0])
        else:
            op_ref[...] = y

    @pl.when(i == 0)
    def _():
        x = xp_ref[...]
        ub0_ref[...] = _rms_norm(x, g_ref[...]).astype(jnp.bfloat16)
        acc0_ref[...] = x

    @pl.when(i < n_cast)
    def _():
        up_ref[i] = up_f32_ref[...].astype(jnp.bfloat16)
        down_ref[i] = down_f32_ref[...].astype(jnp.bfloat16)
        acc0_ref[...] += _mlp_chunk(ub0_ref[...], up_ref, down_ref, i)

    @pl.when(i == n_cast - 1)
    def _():
        y = acc0_ref[...]
        write_prompt(_rms_norm(y, gf_ref[...]) if final_norm else y)

    @pl.when(jnp.logical_and(i >= n_cast, i < n_cast + n_tiles - 1))
    def _():
        write_prompt(body(xp_ref[...]))

    @pl.when(i == n_cast + n_tiles - 1)
    def _():
        os_ref[...] = body(xs_ref[...])


def _mlp(xp, xs, gain, w_up, w_down, gain_final, *, layer, batch, seq, final_norm):
    dec = xs.shape[0]
    m = ROW_TILE
    steps = m // batch
    n_tiles = (seq * batch) // m
    n_cast = D_FF // W_CHUNK
    scratch = [
        pltpu.VMEM((n_cast, D_MODEL, W_CHUNK), jnp.bfloat16),
        pltpu.VMEM((n_cast, W_CHUNK, D_MODEL), jnp.bfloat16),
        pltpu.VMEM((m, D_MODEL), jnp.bfloat16),
        pltpu.VMEM((m, D_MODEL), jnp.float32),
    ]
    first = n_cast - 1
    if final_norm:
        scratch.append(pltpu.VMEM((D_MODEL // LANES, m, LANES), jnp.float32))
        op_spec = _natural_spec(batch, steps, first, n_tiles)
        op_shape = jax.ShapeDtypeStruct((batch, seq, D_MODEL), jnp.float32)
    else:
        op_spec = _rows_spec(m, first, n_tiles)
        op_shape = jax.ShapeDtypeStruct((seq * batch, D_MODEL), jnp.float32)
    kern = functools.partial(_mlp_kernel, n_cast=n_cast, n_tiles=n_tiles, final_norm=final_norm)
    return pl.pallas_call(
        kern,
        grid=(n_cast + n_tiles,),
        in_specs=[
            _rows_spec(m, first, n_tiles),
            _const_spec((dec, D_MODEL)),
            _const_spec((None, 1, D_MODEL), (layer, 0, 0)),
            _col_chunk_spec(layer, D_MODEL, n_cast),
            _row_chunk_spec(layer, D_MODEL, n_cast),
            _const_spec((1, D_MODEL)),
        ],
        out_specs=[op_spec, pl.BlockSpec((dec, D_MODEL), lambda i: (0, 0))],
        out_shape=[op_shape, jax.ShapeDtypeStruct((dec, D_MODEL), jnp.float32)],
        scratch_shapes=scratch,
        compiler_params=pltpu.CompilerParams(dimension_semantics=("arbitrary",), vmem_limit_bytes=VMEM_LIMIT),
        name="mlp",
    )(xp, xs, gain, w_up, w_down, gain_final)


def _glu_mlp_kernel(xp_ref, y_ref, xs_ref, gmix_ref, d_ref, wglu_f32_ref, g_ref, up_f32_ref, down_f32_ref,
                    op_ref, os_ref, wglu_ref, up_ref, down_ref, ub0_ref, acc0_ref, *maybe_slab_ref,
                    n_pre, n_cast, n_tiles):
    i = pl.program_id(0)
    body = functools.partial(_mlp_rows, g_ref=g_ref, up_ref=up_ref, down_ref=down_ref, gf_ref=None,
                             final_norm=False)

    def mixed():
        x = _to_time_major(xp_ref, maybe_slab_ref[0]) if maybe_slab_ref else xp_ref[...]
        return _s5_finish_rows(x, y_ref, gmix_ref, d_ref, wglu_ref)

    @pl.when(i < n_pre)
    def _():
        wglu_ref[i] = wglu_f32_ref[...].astype(jnp.bfloat16)

    @pl.when(i == n_pre)
    def _():
        x = mixed()
        ub0_ref[...] = _rms_norm(x, g_ref[...]).astype(jnp.bfloat16)
        acc0_ref[...] = x

    @pl.when(jnp.logical_and(i >= n_pre, i < n_pre + n_cast))
    def _():
        c = i - n_pre
        up_ref[c] = up_f32_ref[...].astype(jnp.bfloat16)
        down_ref[c] = down_f32_ref[...].astype(jnp.bfloat16)
        acc0_ref[...] += _mlp_chunk(ub0_ref[...], up_ref, down_ref, c)

    @pl.when(i == n_pre + n_cast - 1)
    def _():
        op_ref[...] = acc0_ref[...]

    @pl.when(jnp.logical_and(i >= n_pre + n_cast, i < n_pre + n_cast + n_tiles - 1))
    def _():
        op_ref[...] = body(mixed())

    @pl.when(i == n_pre + n_cast + n_tiles - 1)
    def _():
        os_ref[...] = body(xs_ref[...])


def _glu_mlp(xp, y, xs, gain_mix, d_skip, w_glu, gain, w_up, w_down, *, layer, mixer, batch, seq):
    natural = xp.ndim == 3
    dec = xs.shape[0]
    m = ROW_TILE
    n_tiles = (seq * batch) // m
    n_pre = (2 * D_MODEL) // W_CHUNK
    n_cast = D_FF // W_CHUNK
    first = n_pre + n_cast - 1
    xp_spec = (_natural_spec(batch, m // batch, first, n_tiles) if natural else _rows_spec(m, first, n_tiles))
    slab = [pltpu.VMEM((D_MODEL // LANES, m, LANES), jnp.float32)] if natural else []
    return pl.pallas_call(
        functools.partial(_glu_mlp_kernel, n_pre=n_pre, n_cast=n_cast, n_tiles=n_tiles),
        grid=(n_pre + n_cast + n_tiles,),
        in_specs=[
            xp_spec,
            _rows_spec(m, first, n_tiles),
            _const_spec((dec, D_MODEL)),
            _const_spec((None, 1, D_MODEL), (layer, 0, 0)),
            _const_spec((None, 1, D_MODEL), (mixer, 0, 0)),
            _col_chunk_spec(mixer, D_MODEL, n_pre),
            _const_spec((None, 1, D_MODEL), (layer, 0, 0)),
            pl.BlockSpec((None, D_MODEL, W_CHUNK), lambda i: (layer, 0, jnp.clip(i - n_pre, 0, n_cast - 1))),
            pl.BlockSpec((None, W_CHUNK, D_MODEL), lambda i: (layer, jnp.clip(i - n_pre, 0, n_cast - 1), 0)),
        ],
        out_specs=[_rows_spec(m, first, n_tiles), pl.BlockSpec((dec, D_MODEL), lambda i: (0, 0))],
        out_shape=[jax.ShapeDtypeStruct((seq * batch, D_MODEL), jnp.float32),
                   jax.ShapeDtypeStruct((dec, D_MODEL), jnp.float32)],
        scratch_shapes=[
            pltpu.VMEM((n_pre, D_MODEL, W_CHUNK), jnp.bfloat16),
            pltpu.VMEM((n_cast, D_MODEL, W_CHUNK), jnp.bfloat16),
            pltpu.VMEM((n_cast, W_CHUNK, D_MODEL), jnp.bfloat16),
            pltpu.VMEM((m, D_MODEL), jnp.bfloat16),
            pltpu.VMEM((m, D_MODEL), jnp.float32),
        ] + slab,
        compiler_params=pltpu.CompilerParams(dimension_semantics=("arbitrary",), vmem_limit_bytes=VMEM_LIMIT),
        name="glu_mlp",
    )(xp, y, xs, gain_mix, d_skip, w_glu, gain, w_up, w_down)


def _s5_discretise(a_re, a_im, log_dt, b_re, b_im):
    dt = jnp.exp(log_dt)[:, None]
    k = jnp.arange(CHUNK + 1, dtype=jnp.float32)[:, None, None]
    mag = jnp.exp(k * (a_re * dt))
    pw_re = mag * jnp.cos(k * (a_im * dt))
    pw_im = mag * jnp.sin(k * (a_im * dt))
    ab_re, ab_im = pw_re[1], pw_im[1]
    den = a_re * a_re + a_im * a_im
    q_re = ((ab_re - 1.0) * a_re + ab_im * a_im) / den
    q_im = (ab_im * a_re - (ab_re - 1.0) * a_im) / den
    bb_re = q_re[..., None] * b_re - q_im[..., None] * b_im
    bb_im = q_re[..., None] * b_im + q_im[..., None] * b_re
    return pw_re, pw_im, bb_re, bb_im


def _s5_step_params(pw_re, pw_im, bb_re, bb_im, c_re, c_im):
    gps = MXU_K // GROUP_SIZE
    eye = jnp.eye(gps, dtype=jnp.float32)

    def pack_b(w):
        w = w.reshape(N_SLABS, gps, STATE_DIM, GROUP_SIZE)
        return jnp.einsum('sgph,gk->sghkp', w, eye).reshape(N_SLABS, MXU_K, SLAB_STATE)

    def pack_c(w):
        w = w.reshape(N_SLABS, gps, GROUP_SIZE, STATE_DIM)
        return jnp.einsum('sghp,gk->sgpkh', w, eye).reshape(N_SLABS, SLAB_STATE, MXU_K)

    wb = jnp.concatenate([pack_b(bb_re), pack_b(bb_im)], axis=2).astype(jnp.bfloat16)
    wc = jnp.concatenate([pack_c(c_re), pack_c(-c_im)], axis=1).astype(jnp.bfloat16)
    return pw_re[1].reshape(1, N_STATE), pw_im[1].reshape(1, N_STATE), wb, wc


def _s5_chunk_params(pw_re, pw_im, bb_re, bb_im, c_re, c_im):
    ps_re = jnp.transpose(pw_re[CHUNK - 1::-1], (1, 0, 2))[:, :, None, :]
    ps_im = jnp.transpose(pw_im[CHUNK - 1::-1], (1, 0, 2))[:, :, None, :]
    bt_re = jnp.transpose(bb_re, (0, 2, 1))[:, None]
    bt_im = jnp.transpose(bb_im, (0, 2, 1))[:, None]
    ws = jnp.concatenate([ps_re * bt_re - ps_im * bt_im, ps_re * bt_im + ps_im * bt_re],
                         axis=-1).reshape(N_GROUPS, MXU_K, 2 * STATE_DIM)
    pt_re = jnp.transpose(pw_re, (1, 2, 0))
    pt_im = jnp.transpose(pw_im, (1, 2, 0))
    ct_re = jnp.transpose(c_re, (0, 2, 1))[:, :, None, :]
    ct_im = jnp.transpose(c_im, (0, 2, 1))[:, :, None, :]

    def ca(t0):
        p_r = pt_re[:, :, t0:t0 + CHUNK, None]
        p_i = pt_im[:, :, t0:t0 + CHUNK, None]
        return ((ct_re * p_r - ct_im * p_i).reshape(N_GROUPS, STATE_DIM, MXU_K),
                (ct_re * p_i + ct_im * p_r).reshape(N_GROUPS, STATE_DIM, MXU_K))

    ca0_re, ca0_im = ca(0)
    ca1_re, ca1_im = ca(1)
    hi = lax.Precision.HIGHEST
    kc = (jnp.einsum('gph,gpx->ghx', bb_re, ca0_re, precision=hi)
          - jnp.einsum('gph,gpx->ghx', bb_im, ca0_im, precision=hi))
    wy = jnp.concatenate([ca1_re, -ca1_im], axis=1)
    a1 = jnp.concatenate([pw_re[CHUNK], pw_re[CHUNK]], axis=-1)
    a2 = jnp.concatenate([-pw_im[CHUNK], pw_im[CHUNK]], axis=-1)
    return ws.astype(jnp.bfloat16), kc, wy.astype(jnp.bfloat16), a1, a2


def kernel(x_prompt, x_sample, state_ssm_re, state_ssm_im, state_conv, norm_mix, norm_mlp, norm_final,
           ssm_a_re, ssm_a_im, ssm_log_dt, ssm_b_re, ssm_b_im, ssm_c_re, ssm_c_im, ssm_d, ssm_w_glu,
           conv_w_in, conv_w, conv_w_out, mlp_w_up, mlp_w_down):
    batch, seq, _ = x_prompt.shape
    dec = x_sample.shape[0]
    n_ssm = ssm_a_re.shape[0]
    n_conv = conv_w_in.shape[0]

    g_mix = norm_mix.reshape(DEPTH, 1, D_MODEL)
    g_mlp = norm_mlp.reshape(DEPTH, 1, D_MODEL)
    g_final = norm_final.reshape(1, D_MODEL)
    d_skip = ssm_d.reshape(n_ssm, 1, D_MODEL)
    step_params, chunk_params = [], []
    for j in range(n_ssm):
        disc = _s5_discretise(ssm_a_re[j], ssm_a_im[j], ssm_log_dt[j], ssm_b_re[j], ssm_b_im[j])
        step_params.append(_s5_step_params(*disc, ssm_c_re[j], ssm_c_im[j]))
        chunk_params.append(_s5_chunk_params(*disc, ssm_c_re[j], ssm_c_im[j]))
    h0_re = state_ssm_re.reshape(n_ssm, dec, N_STATE)
    h0_im = state_ssm_im.reshape(n_ssm, dec, N_STATE)
    buf_s = jnp.transpose(state_conv, (0, 2, 1, 3)).reshape(n_conv, HIST * dec, D_MODEL)

    xp = x_prompt
    xs = x_sample.reshape(dec, D_MODEL)
    pre, pim, sre, sim, pbuf, sbuf = [], [], [], [], [], []
    for i in range(DEPTH):
        j = i // 2
        if i % 2 == 0:
            y, a, b = _s5_core(xp, g_mix, chunk_params[j], layer=i, batch=batch, seq=seq)
            xs, c, d = _s5_sample(xs, h0_re, h0_im, g_mix, step_params[j], d_skip, ssm_w_glu, layer=i, mixer=j)
            pre.append(a), pim.append(b), sre.append(c), sim.append(d)
            xp, xs = _glu_mlp(xp, y, xs, g_mix, d_skip, ssm_w_glu, g_mlp, mlp_w_up, mlp_w_down,
                              layer=i, mixer=j, batch=batch, seq=seq)
        else:
            xp, xs, a, b = _conv_mixer(xp, xs, buf_s, g_mix, conv_w_in, conv_w, conv_w_out,
                                       layer=i, mixer=j, batch=batch, seq=seq)
            pbuf.append(a), sbuf.append(b)
            xp, xs = _mlp(xp, xs, g_mlp, mlp_w_up, mlp_w_down, g_final, layer=i, batch=batch, seq=seq,
                          final_norm=(i == DEPTH - 1))

    def states(parts, n):
        return jnp.stack(parts).reshape(n_ssm, n, N_GROUPS, STATE_DIM)

    def bufs(parts, n):
        return jnp.transpose(jnp.stack(parts).reshape(n_conv, HIST, n, D_MODEL), (0, 2, 1, 3))

    return (xp, xs.reshape(dec, 1, D_MODEL), states(pre, batch), states(pim, batch), bufs(pbuf, batch),
            states(sre, dec), states(sim, dec), bufs(sbuf, dec))
```

```python
import functools

import jax
import jax.numpy as jnp
from jax import lax
from jax.experimental import pallas as pl
from jax.experimental.pallas import tpu as pltpu

D_MODEL = 1024
DEPTH = 4
GROUP_SIZE = 16
N_GROUPS = D_MODEL // GROUP_SIZE
STATE_DIM = 64
N_STATE = N_GROUPS * STATE_DIM
CONV_WIDTH = 3
HIST = CONV_WIDTH - 1
D_FF = 4 * D_MODEL
RMS_EPS = 1e-6

LANES = 128
MXU_K = 256
N_SLABS = D_MODEL // MXU_K
SLAB_STATE = N_STATE // N_SLABS
CHUNK = MXU_K // GROUP_SIZE
W_CHUNK = 512
VMEM_LIMIT = 58 * 1024 * 1024
ROW_TILE = 512
S5_TILE = 1024
GLU_ROWS = 256
PIPE_COLS = 2


def _rms_norm(x, g):
    ms = jnp.mean(x * x, axis=-1, keepdims=True)
    return x * lax.rsqrt(ms + RMS_EPS) * g


def _const_spec(shape, index=None):
    index = (0,) * len(shape) if index is None else index
    return pl.BlockSpec(shape, lambda i: index, pipeline_mode=pl.Buffered(1))


def _rows_spec(m, first_step, n_tiles):
    return pl.BlockSpec((m, D_MODEL), lambda i: (jnp.clip(i - first_step, 0, n_tiles - 1), 0))


def _natural_spec(batch, steps, first_step, n_tiles):
    return pl.BlockSpec((batch, steps, D_MODEL),
                        lambda i: (0, jnp.clip(i - first_step, 0, n_tiles - 1), 0))


def _col_chunk_spec(layer, n_rows, n_chunks):
    return pl.BlockSpec((None, n_rows, W_CHUNK), lambda i: (layer, 0, jnp.minimum(i, n_chunks - 1)))


def _row_chunk_spec(layer, n_cols, n_chunks):
    return pl.BlockSpec((None, W_CHUNK, n_cols), lambda i: (layer, jnp.minimum(i, n_chunks - 1), 0))


def _to_time_major(x_ref, slab_ref):
    batch, steps, _ = x_ref.shape
    for c in range(D_MODEL // LANES):
        for b in range(batch):
            slab_ref[c, pl.ds(b, steps, stride=batch), :] = x_ref[b, :, c * LANES:(c + 1) * LANES]
    return jnp.concatenate([slab_ref[c] for c in range(D_MODEL // LANES)], axis=1)


def _from_time_major(y, o_ref, slab_ref):
    batch, steps, _ = o_ref.shape
    for c in range(D_MODEL // LANES):
        slab_ref[c] = y[:, c * LANES:(c + 1) * LANES]
        for b in range(batch):
            o_ref[b, :, c * LANES:(c + 1) * LANES] = slab_ref[c, pl.ds(b, steps, stride=batch), :]


def _glu_tail(x, yb, wglu_ref):
    n_half = D_MODEL // W_CHUNK
    outs = []
    for c in range(n_half):
        z_lin = jnp.dot(yb, wglu_ref[c], preferred_element_type=jnp.float32)
        z_gate = jnp.dot(yb, wglu_ref[n_half + c], preferred_element_type=jnp.float32)
        outs.append(z_lin * jax.nn.sigmoid(z_gate))
    return x + jnp.concatenate(outs, axis=1)


def _s5_step_rows(x, hre_ref, him_ref, g_ref, are_ref, aim_ref, wb_ref, wc_ref, d_ref, wglu_ref,
                  bu_ref, hb_ref, y_ref):
    u = _rms_norm(x, g_ref[...])
    ub = u.astype(jnp.bfloat16)
    for s in range(N_SLABS):
        ch = slice(s * MXU_K, (s + 1) * MXU_K)
        bu_ref[...] = jnp.dot(ub[:, ch], wb_ref[s], preferred_element_type=jnp.float32)
        for l0 in range(0, SLAB_STATE, LANES):
            re_l = slice(l0, l0 + LANES)
            im_l = slice(SLAB_STATE + l0, SLAB_STATE + l0 + LANES)
            st_l = slice(s * SLAB_STATE + l0, s * SLAB_STATE + l0 + LANES)
            a_r, a_i = are_ref[:, st_l], aim_ref[:, st_l]
            h_r, h_i = hre_ref[:, st_l], him_ref[:, st_l]
            n_r = a_r * h_r - a_i * h_i + bu_ref[:, re_l]
            n_i = a_r * h_i + a_i * h_r + bu_ref[:, im_l]
            hre_ref[:, st_l] = n_r
            him_ref[:, st_l] = n_i
            hb_ref[:, re_l] = n_r.astype(jnp.bfloat16)
            hb_ref[:, im_l] = n_i.astype(jnp.bfloat16)
        ys = jnp.dot(hb_ref[...], wc_ref[s], preferred_element_type=jnp.float32)
        ys = ys + d_ref[:, ch] * u[:, ch]
        y_ref[:, ch] = jax.nn.gelu(ys).astype(jnp.bfloat16)
    return _glu_tail(x, y_ref[...], wglu_ref)


def _s5_sample_kernel(xs_ref, h0re_ref, h0im_ref, g_ref, are_ref, aim_ref, wb_ref, wc_ref, d_ref, wglu_f32_ref,
                      os_ref, sre_ref, sim_ref, wglu_ref, bu_ref, hb_ref, y_ref, *, n_cast):
    i = pl.program_id(0)

    @pl.when(i < n_cast)
    def _():
        wglu_ref[i] = wglu_f32_ref[...].astype(jnp.bfloat16)

    @pl.when(i == n_cast)
    def _():
        sre_ref[...] = h0re_ref[...]
        sim_ref[...] = h0im_ref[...]
        os_ref[...] = _s5_step_rows(xs_ref[...], sre_ref, sim_ref, g_ref, are_ref, aim_ref, wb_ref, wc_ref,
                                    d_ref, wglu_ref, bu_ref, hb_ref, y_ref)


def _s5_sample(xs, h0_re, h0_im, gain, step_params, d_skip, w_glu, *, layer, mixer):
    a_re, a_im, wb, wc = step_params
    dec = xs.shape[0]
    n_cast = (2 * D_MODEL) // W_CHUNK
    return pl.pallas_call(
        functools.partial(_s5_sample_kernel, n_cast=n_cast),
        grid=(n_cast + 1,),
        in_specs=[
            _const_spec((dec, D_MODEL)),
            _const_spec((None, dec, N_STATE), (mixer, 0, 0)), _const_spec((None, dec, N_STATE), (mixer, 0, 0)),
            _const_spec((None, 1, D_MODEL), (layer, 0, 0)),
            _const_spec((None, 1, N_STATE), (mixer, 0, 0)), _const_spec((None, 1, N_STATE), (mixer, 0, 0)),
            _const_spec((None, N_SLABS, MXU_K, 2 * SLAB_STATE), (mixer, 0, 0, 0)),
            _const_spec((None, N_SLABS, 2 * SLAB_STATE, MXU_K), (mixer, 0, 0, 0)),
            _const_spec((None, 1, D_MODEL), (mixer, 0, 0)),
            _col_chunk_spec(mixer, D_MODEL, n_cast),
        ],
        out_specs=[
            pl.BlockSpec((dec, D_MODEL), lambda i: (0, 0)),
            pl.BlockSpec((dec, N_STATE), lambda i: (0, 0)), pl.BlockSpec((dec, N_STATE), lambda i: (0, 0)),
        ],
        out_shape=[
            jax.ShapeDtypeStruct((dec, D_MODEL), jnp.float32),
            jax.ShapeDtypeStruct((dec, N_STATE), jnp.float32), jax.ShapeDtypeStruct((dec, N_STATE), jnp.float32),
        ],
        scratch_shapes=[
            pltpu.VMEM((n_cast, D_MODEL, W_CHUNK), jnp.bfloat16),
            pltpu.VMEM((dec, 2 * SLAB_STATE), jnp.float32),
            pltpu.VMEM((dec, 2 * SLAB_STATE), jnp.bfloat16),
            pltpu.VMEM((dec, D_MODEL), jnp.bfloat16),
        ],
        compiler_params=pltpu.CompilerParams(dimension_semantics=("arbitrary",), vmem_limit_bytes=VMEM_LIMIT),
        name="s5_sample",
    )(xs, h0_re, h0_im, gain, a_re, a_im, wb, wc, d_skip, w_glu)


def _block_transpose(sets, lane_block):
    sets = [list(vs) for vs in sets]
    n = len(sets[0])
    d = 1
    while d < n:
        upper = (lane_block & d) != 0
        for vs in sets:
            for i in range(n):
                if i & d == 0:
                    a, b = vs[i], vs[i + d]
                    vs[i] = jnp.where(upper, pltpu.roll(b, GROUP_SIZE * d, 1), a)
                    vs[i + d] = jnp.where(upper, b, pltpu.roll(a, LANES - GROUP_SIZE * d, 1))
        d *= 2
    return sets


def _s5_chunk_rows(x, hst_ref, g_ref, ws_ref, wt_ref, wy_ref, a1_ref, a2_ref, y_ref,
                   lhs_ref, sc_ref, hp_ref, yg_ref, *, batch):
    m = x.shape[0]
    n_chunks = m // (CHUNK * batch)
    blocks = LANES // GROUP_SIZE
    n_cols = D_MODEL // LANES
    n_halves = CHUNK // blocks
    pass_chunks = min(n_chunks, 4)
    pass_rows = pass_chunks * batch
    u = _rms_norm(x, g_ref[...])
    lane_block = lax.broadcasted_iota(jnp.int32, (pass_rows, LANES), 1) // GROUP_SIZE

    def step_rows(c, s):
        r0 = (c * CHUNK + s) * batch
        return slice(r0, r0 + batch)

    def gather(qs):
        for c0 in range(0, n_chunks, pass_chunks):
            prow = slice(c0 * batch, c0 * batch + pass_rows)
            sets = [[jnp.concatenate([u[step_rows(c0 + c, k * blocks + s), q * LANES:(q + 1) * LANES]
                                      for c in range(pass_chunks)], axis=0).astype(jnp.bfloat16)
                     for s in range(blocks)]
                    for q in qs for k in range(n_halves)]
            w = _block_transpose(sets, lane_block)
            for iq, q in enumerate(qs):
                for j in range(blocks):
                    lhs = jnp.concatenate([w[iq * n_halves + k][j] for k in range(n_halves)], axis=1)
                    lhs_ref[q * blocks + j, prow, :] = lhs

    def project(gs):
        for g in gs:
            sc_ref[g] = jnp.dot(lhs_ref[g], ws_ref[g], preferred_element_type=jnp.float32)
        for g in gs:
            recur(g)
        for g in gs:
            yg_ref[g] = (jnp.dot(lhs_ref[g], wt_ref[g], preferred_element_type=jnp.float32)
                         + jnp.dot(hp_ref[g], wy_ref[g], preferred_element_type=jnp.float32)
                         ).astype(jnp.bfloat16)

    def recur(g):
        a1 = a1_ref[g:g + 1, :]
        a2 = a2_ref[g:g + 1, :]
        sc = sc_ref[g]
        sc_sw = pltpu.roll(sc, STATE_DIM, 1)
        h = hst_ref[g]
        hs = pltpu.roll(h, STATE_DIM, 1)
        h_prev = []
        for c in range(n_chunks):
            h_prev.append(h)
            rows_c = slice(c * batch, (c + 1) * batch)
            h, hs = a1 * h + a2 * hs + sc[rows_c], a1 * hs - a2 * h + sc_sw[rows_c]
        hst_ref[g] = h
        hp_ref[g] = jnp.concatenate(h_prev, axis=0).astype(jnp.bfloat16)

    def scatter(qs):
        for c0 in range(0, n_chunks, pass_chunks):
            prow = slice(c0 * batch, c0 * batch + pass_rows)
            sets = [[yg_ref[q * blocks + j, prow, k * LANES:(k + 1) * LANES] for j in range(blocks)]
                    for q in qs for k in range(n_halves)]
            z = _block_transpose(sets, lane_block)
            for iq, q in enumerate(qs):
                for k in range(n_halves):
                    for s in range(blocks):
                        zs = z[iq * n_halves + k][s].astype(jnp.float32)
                        for c in range(pass_chunks):
                            y_ref[step_rows(c0 + c, k * blocks + s), q * LANES:(q + 1) * LANES] = (
                                zs[c * batch:(c + 1) * batch])

    col_blocks = [list(range(q0, q0 + PIPE_COLS)) for q0 in range(0, n_cols, PIPE_COLS)]
    grp_blocks = [[q * blocks + j for q in qs for j in range(blocks)] for qs in col_blocks]
    n_blk = len(col_blocks)
    for step in range(n_blk + 2):
        if step < n_blk:
            gather(col_blocks[step])
        if 1 <= step <= n_blk:
            project(grp_blocks[step - 1])
        if step >= 2:
            scatter(col_blocks[step - 2])


def _s5_core_kernel(xp_ref, g_ref, ws_ref, kc_ref, wy_ref, a1_ref, a2_ref,
                    y_ref, pre_ref, pim_ref, wt_ref, hst_ref, lhs_ref, sc_ref, hp_ref, yg_ref,
                    *maybe_slab_ref, n_tiles, batch):
    i = pl.program_id(0)

    @pl.when(i == 0)
    def _():
        hst_ref[...] = jnp.zeros_like(hst_ref)
        lane = lax.broadcasted_iota(jnp.int32, (GROUP_SIZE, MXU_K), 1)

        def build(g, carry):
            kc = kc_ref[g]
            for s in range(CHUNK):
                blk = kc if s == 0 else jnp.where(lane >= s * GROUP_SIZE, pltpu.roll(kc, s * GROUP_SIZE, 1), 0.0)
                wt_ref[g, s * GROUP_SIZE:(s + 1) * GROUP_SIZE, :] = blk.astype(jnp.bfloat16)
            return carry

        lax.fori_loop(0, N_GROUPS, build, 0)

    x = _to_time_major(xp_ref, maybe_slab_ref[0]) if maybe_slab_ref else xp_ref[...]
    _s5_chunk_rows(x, hst_ref, g_ref, ws_ref, wt_ref, wy_ref, a1_ref, a2_ref,
                   y_ref, lhs_ref, sc_ref, hp_ref, yg_ref, batch=batch)

    @pl.when(i == n_tiles - 1)
    def _():
        low = lax.broadcasted_iota(jnp.int32, (batch, LANES), 1) < STATE_DIM
        for g in range(0, N_GROUPS, 2):
            h0, h1 = hst_ref[g], hst_ref[g + 1]
            cols = slice(g * STATE_DIM, (g + 2) * STATE_DIM)
            pre_ref[:, cols] = jnp.where(low, h0, pltpu.roll(h1, STATE_DIM, 1))
            pim_ref[:, cols] = jnp.where(low, pltpu.roll(h0, STATE_DIM, 1), h1)


def _s5_core(xp, gain, chunk_params, *, layer, mixer, batch, seq):
    natural = xp.ndim == 3
    ws, kc, wy, a1, a2 = chunk_params
    m = S5_TILE
    n_tiles = (seq * batch) // m

    def xp_spec(first_step):
        return (_natural_spec(batch, m // batch, first_step, n_tiles) if natural
                else _rows_spec(m, first_step, n_tiles))

    slab = [pltpu.VMEM((D_MODEL // LANES, m, LANES), jnp.float32)] if natural else []
    scratch = [
        pltpu.VMEM((N_GROUPS, MXU_K, MXU_K), jnp.bfloat16),
        pltpu.VMEM((N_GROUPS, batch, 2 * STATE_DIM), jnp.float32),
        pltpu.VMEM((N_GROUPS, m // CHUNK, MXU_K), jnp.bfloat16),
        pltpu.VMEM((N_GROUPS, m // CHUNK, 2 * STATE_DIM), jnp.float32),
        pltpu.VMEM((N_GROUPS, m // CHUNK, 2 * STATE_DIM), jnp.bfloat16),
        pltpu.VMEM((N_GROUPS, m // CHUNK, MXU_K), jnp.bfloat16),
    ]
    y, pre, pim = pl.pallas_call(
        functools.partial(_s5_core_kernel, n_tiles=n_tiles, batch=batch),
        grid=(n_tiles,),
        in_specs=[
            xp_spec(0),
            _const_spec((None, 1, D_MODEL), (layer, 0, 0)),
            _const_spec((None, N_GROUPS, MXU_K, 2 * STATE_DIM), (mixer, 0, 0, 0)),
            _const_spec((None, N_GROUPS, GROUP_SIZE, MXU_K), (mixer, 0, 0, 0)),
            _const_spec((None, N_GROUPS, 2 * STATE_DIM, MXU_K), (mixer, 0, 0, 0)),
            _const_spec((None, N_GROUPS, 2 * STATE_DIM), (mixer, 0, 0)),
            _const_spec((None, N_GROUPS, 2 * STATE_DIM), (mixer, 0, 0)),
        ],
        out_specs=[
            _rows_spec(m, 0, n_tiles),
            pl.BlockSpec((batch, N_STATE), lambda i: (0, 0)), pl.BlockSpec((batch, N_STATE), lambda i: (0, 0)),
        ],
        out_shape=[
            jax.ShapeDtypeStruct((seq * batch, D_MODEL), jnp.float32),
            jax.ShapeDtypeStruct((batch, N_STATE), jnp.float32), jax.ShapeDtypeStruct((batch, N_STATE), jnp.float32),
        ],
        scratch_shapes=scratch + slab,
        compiler_params=pltpu.CompilerParams(dimension_semantics=("arbitrary",), vmem_limit_bytes=VMEM_LIMIT),
        name="s5_core",
    )(xp, gain, ws, kc, wy, a1, a2)
    return y, pre, pim


def _s5_finish_rows(x, y_ref, g_ref, d_ref, wglu_ref):
    parts = []
    for r0 in range(0, x.shape[0], GLU_ROWS):
        rows = slice(r0, r0 + GLU_ROWS)
        ys = y_ref[rows, :] + d_ref[...] * _rms_norm(x[rows], g_ref[...])
        parts.append(_glu_tail(x[rows], jax.nn.gelu(ys).astype(jnp.bfloat16), wglu_ref))
    return jnp.concatenate(parts, axis=0)


def _conv_rows(x, g_ref, win_ref, cw_ref, wout_ref, cvx_ref, *, rows_per_step):
    br = rows_per_step
    m = x.shape[0]
    hist = HIST * br
    n_d = D_MODEL // W_CHUNK
    ub = _rms_norm(x, g_ref[...]).astype(jnp.bfloat16)
    bcv = [jnp.dot(ub, win_ref[c], preferred_element_type=jnp.float32) for c in range(3 * n_d)]
    outs = 0.0
    for c in range(n_d):
        cols = slice(c * W_CHUNK, (c + 1) * W_CHUNK)
        cv = bcv[n_d + c] * bcv[2 * n_d + c]
        cvx_ref[hist:hist + m, cols] = cv
        y = cw_ref[CONV_WIDTH - 1:CONV_WIDTH, cols] * cv
        for k in range(CONV_WIDTH - 1):
            y = y + cw_ref[k:k + 1, cols] * cvx_ref[k * br:k * br + m, cols]
        gated = (bcv[c] * y).astype(jnp.bfloat16)
        outs = outs + jnp.dot(gated, wout_ref[c], preferred_element_type=jnp.float32)
    cvx_ref[0:hist, :] = cvx_ref[m:m + hist, :]
    return x + outs


def _conv_kernel(xp_ref, xs_ref, bufs_ref, g_ref, win_f32_ref, cw_ref, wout_f32_ref,
                 op_ref, os_ref, nbufp_ref, nbufs_ref, win_ref, wout_ref, cvx_ref, *, n_cast, n_tiles, batch):
    i = pl.program_id(0)
    dec = xs_ref.shape[0]
    body = functools.partial(_conv_rows, g_ref=g_ref, win_ref=win_ref, cw_ref=cw_ref, wout_ref=wout_ref,
                             cvx_ref=cvx_ref)

    @pl.when(i < n_cast)
    def _():
        win_ref[i] = win_f32_ref[...].astype(jnp.bfloat16)

    @pl.when(i < wout_ref.shape[0])
    def _():
        wout_ref[i] = wout_f32_ref[...].astype(jnp.bfloat16)

    @pl.when(i == n_cast)
    def _():
        cvx_ref[0:HIST * batch, :] = jnp.zeros((HIST * batch, D_MODEL), jnp.float32)

    @pl.when(jnp.logical_and(i >= n_cast, i < n_cast + n_tiles))
    def _():
        op_ref[...] = body(xp_ref[...], rows_per_step=batch)

    @pl.when(i == n_cast + n_tiles - 1)
    def _():
        nbufp_ref[...] = cvx_ref[0:HIST * batch, :]

    @pl.when(i == n_cast + n_tiles)
    def _():
        cvx_ref[0:HIST * dec, :] = bufs_ref[...]
        os_ref[...] = body(xs_ref[...], rows_per_step=dec)
        nbufs_ref[...] = cvx_ref[0:HIST * dec, :]


def _conv_mixer(xp, xs, buf_s, gain, w_in, conv_w, w_out, *, layer, mixer, batch, seq):
    dec = xs.shape[0]
    m = ROW_TILE
    n_tiles = (seq * batch) // m
    n_cast = (3 * D_MODEL) // W_CHUNK
    n_out = D_MODEL // W_CHUNK
    kern = functools.partial(_conv_kernel, n_cast=n_cast, n_tiles=n_tiles, batch=batch)
    return pl.pallas_call(
        kern,
        grid=(n_cast + n_tiles + 1,),
        in_specs=[
            _rows_spec(m, n_cast, n_tiles),
            _const_spec((dec, D_MODEL)),
            _const_spec((None, HIST * dec, D_MODEL), (mixer, 0, 0)),
            _const_spec((None, 1, D_MODEL), (layer, 0, 0)),
            _col_chunk_spec(mixer, D_MODEL, n_cast),
            _const_spec((None, CONV_WIDTH, D_MODEL), (mixer, 0, 0)),
            _row_chunk_spec(mixer, D_MODEL, n_out),
        ],
        out_specs=[
            _rows_spec(m, n_cast, n_tiles),
            pl.BlockSpec((dec, D_MODEL), lambda i: (0, 0)),
            pl.BlockSpec((HIST * batch, D_MODEL), lambda i: (0, 0)),
            pl.BlockSpec((HIST * dec, D_MODEL), lambda i: (0, 0)),
        ],
        out_shape=[
            jax.ShapeDtypeStruct((seq * batch, D_MODEL), jnp.float32),
            jax.ShapeDtypeStruct((dec, D_MODEL), jnp.float32),
            jax.ShapeDtypeStruct((HIST * batch, D_MODEL), jnp.float32),
            jax.ShapeDtypeStruct((HIST * dec, D_MODEL), jnp.float32),
        ],
        scratch_shapes=[
            pltpu.VMEM((n_cast, D_MODEL, W_CHUNK), jnp.bfloat16),
            pltpu.VMEM((n_out, W_CHUNK, D_MODEL), jnp.bfloat16),
            pltpu.VMEM((max(HIST * batch + m, (HIST + 1) * dec), D_MODEL), jnp.float32),
        ],
        compiler_params=pltpu.CompilerParams(dimension_semantics=("arbitrary",), vmem_limit_bytes=VMEM_LIMIT),
        name="conv_mixer",
    )(xp, xs, buf_s, gain, w_in, conv_w, w_out)


def _mlp_chunk(ub, up_ref, down_ref, c):
    h = jnp.dot(ub, up_ref[c], preferred_element_type=jnp.float32)
    h = jnp.maximum(h, 0.0)
    h = (h * h).astype(jnp.bfloat16)
    return jnp.dot(h, down_ref[c], preferred_element_type=jnp.float32)


def _mlp_rows(x, g_ref, up_ref, down_ref, gf_ref, *, final_norm):
    ub = _rms_norm(x, g_ref[...]).astype(jnp.bfloat16)
    acc = x
    for c in range(D_FF // W_CHUNK):
        acc = acc + _mlp_chunk(ub, up_ref, down_ref, c)
    if final_norm:
        acc = _rms_norm(acc, gf_ref[...])
    return acc


def _mlp_kernel(xp_ref, xs_ref, g_ref, up_f32_ref, down_f32_ref, gf_ref, op_ref, os_ref,
                up_ref, down_ref, ub0_ref, acc0_ref, *maybe_slab_ref, n_cast, n_tiles, final_norm):
    i = pl.program_id(0)
    body = functools.partial(_mlp_rows, g_ref=g_ref, up_ref=up_ref, down_ref=down_ref, gf_ref=gf_ref,
                             final_norm=final_norm)

    def write_prompt(y):
        if maybe_slab_ref:
            _from_time_major(y, op_ref, maybe_slab_ref[0])
        else:
            op_ref[...] = y

    @pl.when(i == 0)
    def _():
        x = xp_ref[...]
        ub0_ref[...] = _rms_norm(x, g_ref[...]).astype(jnp.bfloat16)
        acc0_ref[...] = x

    @pl.when(i < n_cast)
    def _():
        up_ref[i] = up_f32_ref[...].astype(jnp.bfloat16)
        down_ref[i] = down_f32_ref[...].astype(jnp.bfloat16)
        acc0_ref[...] += _mlp_chunk(ub0_ref[...], up_ref, down_ref, i)

    @pl.when(i == n_cast - 1)
    def _():
        y = acc0_ref[...]
        write_prompt(_rms_norm(y, gf_ref[...]) if final_norm else y)

    @pl.when(jnp.logical_and(i >= n_cast, i < n_cast + n_tiles - 1))
    def _():
        write_prompt(body(xp_ref[...]))

    @pl.when(i == n_cast + n_tiles - 1)
    def _():
        os_ref[...] = body(xs_ref[...])


def _mlp(xp, xs, gain, w_up, w_down, gain_final, *, layer, batch, seq, final_norm):
    dec = xs.shape[0]
    m = ROW_TILE
    steps = m // batch
    n_tiles = (seq * batch) // m
    n_cast = D_FF // W_CHUNK
    scratch = [
        pltpu.VMEM((n_cast, D_MODEL, W_CHUNK), jnp.bfloat16),
        pltpu.VMEM((n_cast, W_CHUNK, D_MODEL), jnp.bfloat16),
        pltpu.VMEM((m, D_MODEL), jnp.bfloat16),
        pltpu.VMEM((m, D_MODEL), jnp.float32),
    ]
    first = n_cast - 1
    if final_norm:
        scratch.append(pltpu.VMEM((D_MODEL // LANES, m, LANES), jnp.float32))
        op_spec = _natural_spec(batch, steps, first, n_tiles)
        op_shape = jax.ShapeDtypeStruct((batch, seq, D_MODEL), jnp.float32)
    else:
        op_spec = _rows_spec(m, first, n_tiles)
        op_shape = jax.ShapeDtypeStruct((seq * batch, D_MODEL), jnp.float32)
    kern = functools.partial(_mlp_kernel, n_cast=n_cast, n_tiles=n_tiles, final_norm=final_norm)
    return pl.pallas_call(
        kern,
        grid=(n_cast + n_tiles,),
        in_specs=[
            _rows_spec(m, first, n_tiles),
            _const_spec((dec, D_MODEL)),
            _const_spec((None, 1, D_MODEL), (layer, 0, 0)),
            _col_chunk_spec(layer, D_MODEL, n_cast),
            _row_chunk_spec(layer, D_MODEL, n_cast),
            _const_spec((1, D_MODEL)),
        ],
        out_specs=[op_spec, pl.BlockSpec((dec, D_MODEL), lambda i: (0, 0))],
        out_shape=[op_shape, jax.ShapeDtypeStruct((dec, D_MODEL), jnp.float32)],
        scratch_shapes=scratch,
        compiler_params=pltpu.CompilerParams(dimension_semantics=("arbitrary",), vmem_limit_bytes=VMEM_LIMIT),
        name="mlp",
    )(xp, xs, gain, w_up, w_down, gain_final)


def _glu_mlp_kernel(xp_ref, y_ref, xs_ref, gmix_ref, d_ref, wglu_f32_ref, g_ref, up_f32_ref, down_f32_ref,
                    op_ref, os_ref, wglu_ref, up_ref, down_ref, ub0_ref, acc0_ref, *maybe_slab_ref,
                    n_pre, n_cast, n_tiles):
    i = pl.program_id(0)
    body = functools.partial(_mlp_rows, g_ref=g_ref, up_ref=up_ref, down_ref=down_ref, gf_ref=None,
                             final_norm=False)

    def mixed():
        x = _to_time_major(xp_ref, maybe_slab_ref[0]) if maybe_slab_ref else xp_ref[...]
        return _s5_finish_rows(x, y_ref, gmix_ref, d_ref, wglu_ref)

    @pl.when(i < n_pre)
    def _():
        wglu_ref[i] = wglu_f32_ref[...].astype(jnp.bfloat16)

    @pl.when(i == n_pre)
    def _():
        x = mixed()
        ub0_ref[...] = _rms_norm(x, g_ref[...]).astype(jnp.bfloat16)
        acc0_ref[...] = x

    @pl.when(jnp.logical_and(i >= n_pre, i < n_pre + n_cast))
    def _():
        c = i - n_pre
        up_ref[c] = up_f32_ref[...].astype(jnp.bfloat16)
        down_ref[c] = down_f32_ref[...].astype(jnp.bfloat16)
        acc0_ref[...] += _mlp_chunk(ub0_ref[...], up_ref, down_ref, c)

    @pl.when(i == n_pre + n_cast - 1)
    def _():
        op_ref[...] = acc0_ref[...]

    @pl.when(jnp.logical_and(i >= n_pre + n_cast, i < n_pre + n_cast + n_tiles - 1))
    def _():
        op_ref[...] = body(mixed())

    @pl.when(i == n_pre + n_cast + n_tiles - 1)
    def _():
        os_ref[...] = body(xs_ref[...])


def _glu_mlp(xp, y, xs, gain_mix, d_skip, w_glu, gain, w_up, w_down, *, layer, mixer, batch, seq):
    natural = xp.ndim == 3
    dec = xs.shape[0]
    m = ROW_TILE
    n_tiles = (seq * batch) // m
    n_pre = (2 * D_MODEL) // W_CHUNK
    n_cast = D_FF // W_CHUNK
    first = n_pre + n_cast - 1
    xp_spec = (_natural_spec(batch, m // batch, first, n_tiles) if natural else _rows_spec(m, first, n_tiles))
    slab = [pltpu.VMEM((D_MODEL // LANES, m, LANES), jnp.float32)] if natural else []
    return pl.pallas_call(
        functools.partial(_glu_mlp_kernel, n_pre=n_pre, n_cast=n_cast, n_tiles=n_tiles),
        grid=(n_pre + n_cast + n_tiles,),
        in_specs=[
            xp_spec,
            _rows_spec(m, first, n_tiles),
            _const_spec((dec, D_MODEL)),
            _const_spec((None, 1, D_MODEL), (layer, 0, 0)),
            _const_spec((None, 1, D_MODEL), (mixer, 0, 0)),
            _col_chunk_spec(mixer, D_MODEL, n_pre),
            _const_spec((None, 1, D_MODEL), (layer, 0, 0)),
            pl.BlockSpec((None, D_MODEL, W_CHUNK), lambda i: (layer, 0, jnp.clip(i - n_pre, 0, n_cast - 1))),
            pl.BlockSpec((None, W_CHUNK, D_MODEL), lambda i: (layer, jnp.clip(i - n_pre, 0, n_cast - 1), 0)),
        ],
        out_specs=[_rows_spec(m, first, n_tiles), pl.BlockSpec((dec, D_MODEL), lambda i: (0, 0))],
        out_shape=[jax.ShapeDtypeStruct((seq * batch, D_MODEL), jnp.float32),
                   jax.ShapeDtypeStruct((dec, D_MODEL), jnp.float32)],
        scratch_shapes=[
            pltpu.VMEM((n_pre, D_MODEL, W_CHUNK), jnp.bfloat16),
            pltpu.VMEM((n_cast, D_MODEL, W_CHUNK), jnp.bfloat16),
            pltpu.VMEM((n_cast, W_CHUNK, D_MODEL), jnp.bfloat16),
            pltpu.VMEM((m, D_MODEL), jnp.bfloat16),
            pltpu.VMEM((m, D_MODEL), jnp.float32),
        ] + slab,
        compiler_params=pltpu.CompilerParams(dimension_semantics=("arbitrary",), vmem_limit_bytes=VMEM_LIMIT),
        name="glu_mlp",
    )(xp, y, xs, gain_mix, d_skip, w_glu, gain, w_up, w_down)


def _s5_discretise(a_re, a_im, log_dt, b_re, b_im):
    dt = jnp.exp(log_dt)[:, None]
    k = jnp.arange(CHUNK + 1, dtype=jnp.float32)[:, None, None]
    mag = jnp.exp(k * (a_re * dt))
    pw_re = mag * jnp.cos(k * (a_im * dt))
    pw_im = mag * jnp.sin(k * (a_im * dt))
    ab_re, ab_im = pw_re[1], pw_im[1]
    den = a_re * a_re + a_im * a_im
    q_re = ((ab_re - 1.0) * a_re + ab_im * a_im) / den
    q_im = (ab_im * a_re - (ab_re - 1.0) * a_im) / den
    bb_re = q_re[..., None] * b_re - q_im[..., None] * b_im
    bb_im = q_re[..., None] * b_im + q_im[..., None] * b_re
    return pw_re, pw_im, bb_re, bb_im


def _s5_step_params(pw_re, pw_im, bb_re, bb_im, c_re, c_im):
    gps = MXU_K // GROUP_SIZE
    eye = jnp.eye(gps, dtype=jnp.float32)

    def pack_b(w):
        w = w.reshape(N_SLABS, gps, STATE_DIM, GROUP_SIZE)
        return jnp.einsum('sgph,gk->sghkp', w, eye).reshape(N_SLABS, MXU_K, SLAB_STATE)

    def pack_c(w):
        w = w.reshape(N_SLABS, gps, GROUP_SIZE, STATE_DIM)
        return jnp.einsum('sghp,gk->sgpkh', w, eye).reshape(N_SLABS, SLAB_STATE, MXU_K)

    wb = jnp.concatenate([pack_b(bb_re), pack_b(bb_im)], axis=2).astype(jnp.bfloat16)
    wc = jnp.concatenate([pack_c(c_re), pack_c(-c_im)], axis=1).astype(jnp.bfloat16)
    return pw_re[1].reshape(1, N_STATE), pw_im[1].reshape(1, N_STATE), wb, wc


def _s5_chunk_params(pw_re, pw_im, bb_re, bb_im, c_re, c_im):
    ps_re = jnp.transpose(pw_re[CHUNK - 1::-1], (1, 0, 2))[:, :, None, :]
    ps_im = jnp.transpose(pw_im[CHUNK - 1::-1], (1, 0, 2))[:, :, None, :]
    bt_re = jnp.transpose(bb_re, (0, 2, 1))[:, None]
    bt_im = jnp.transpose(bb_im, (0, 2, 1))[:, None]
    ws = jnp.concatenate([ps_re * bt_re - ps_im * bt_im, ps_re * bt_im + ps_im * bt_re],
                         axis=-1).reshape(N_GROUPS, MXU_K, 2 * STATE_DIM)
    pt_re = jnp.transpose(pw_re, (1, 2, 0))
    pt_im = jnp.transpose(pw_im, (1, 2, 0))
    ct_re = jnp.transpose(c_re, (0, 2, 1))[:, :, None, :]
    ct_im = jnp.transpose(c_im, (0, 2, 1))[:, :, None, :]

    def ca(t0):
        p_r = pt_re[:, :, t0:t0 + CHUNK, None]
        p_i = pt_im[:, :, t0:t0 + CHUNK, None]
        return ((ct_re * p_r - ct_im * p_i).reshape(N_GROUPS, STATE_DIM, MXU_K),
                (ct_re * p_i + ct_im * p_r).reshape(N_GROUPS, STATE_DIM, MXU_K))

    ca0_re, ca0_im = ca(0)
    ca1_re, ca1_im = ca(1)
    hi = lax.Precision.HIGHEST
    kc = (jnp.einsum('gph,gpx->ghx', bb_re, ca0_re, precision=hi)
          - jnp.einsum('gph,gpx->ghx', bb_im, ca0_im, precision=hi))
    wy = jnp.concatenate([ca1_re, -ca1_im], axis=1)
    a1 = jnp.concatenate([pw_re[CHUNK], pw_re[CHUNK]], axis=-1)
    a2 = jnp.concatenate([-pw_im[CHUNK], pw_im[CHUNK]], axis=-1)
    return ws.astype(jnp.bfloat16), kc, wy.astype(jnp.bfloat16), a1, a2


def kernel(x_prompt, x_sample, state_ssm_re, state_ssm_im, state_conv, norm_mix, norm_mlp, norm_final,
           ssm_a_re, ssm_a_im, ssm_log_dt, ssm_b_re, ssm_b_im, ssm_c_re, ssm_c_im, ssm_d, ssm_w_glu,
           conv_w_in, conv_w, conv_w_out, mlp_w_up, mlp_w_down):
    batch, seq, _ = x_prompt.shape
    dec = x_sample.shape[0]
    n_ssm = ssm_a_re.shape[0]
    n_conv = conv_w_in.shape[0]

    g_mix = norm_mix.reshape(DEPTH, 1, D_MODEL)
    g_mlp = norm_mlp.reshape(DEPTH, 1, D_MODEL)
    g_final = norm_final.reshape(1, D_MODEL)
    d_skip = ssm_d.reshape(n_ssm, 1, D_MODEL)
    disc = jax.vmap(_s5_discretise)(ssm_a_re, ssm_a_im, ssm_log_dt, ssm_b_re, ssm_b_im)
    step_params = jax.vmap(_s5_step_params)(*disc, ssm_c_re, ssm_c_im)
    chunk_params = jax.vmap(_s5_chunk_params)(*disc, ssm_c_re, ssm_c_im)
    h0_re = state_ssm_re.reshape(n_ssm, dec, N_STATE)
    h0_im = state_ssm_im.reshape(n_ssm, dec, N_STATE)
    buf_s = jnp.transpose(state_conv, (0, 2, 1, 3)).reshape(n_conv, HIST * dec, D_MODEL)

    xp = x_prompt
    xs = x_sample.reshape(dec, D_MODEL)
    pre, pim, sre, sim, pbuf, sbuf = [], [], [], [], [], []
    for i in range(DEPTH):
        j = i // 2
        if i % 2 == 0:
            y, a, b = _s5_core(xp, g_mix, chunk_params, layer=i, mixer=j, batch=batch, seq=seq)
            xs, c, d = _s5_sample(xs, h0_re, h0_im, g_mix, step_params, d_skip, ssm_w_glu, layer=i, mixer=j)
            pre.append(a), pim.append(b), sre.append(c), sim.append(d)
            xp, xs = _glu_mlp(xp, y, xs, g_mix, d_skip, ssm_w_glu, g_mlp, mlp_w_up, mlp_w_down,
                              layer=i, mixer=j, batch=batch, seq=seq)
        else:
            xp, xs, a, b = _conv_mixer(xp, xs, buf_s, g_mix, conv_w_in, conv_w, conv_w_out,
                                       layer=i, mixer=j, batch=batch, seq=seq)
            pbuf.append(a), sbuf.append(b)
            xp, xs = _mlp(xp, xs, g_mlp, mlp_w_up, mlp_w_down, g_final, layer=i, batch=batch, seq=seq,
                          final_norm=(i == DEPTH - 1))

    def states(parts, n):
        return jnp.stack(parts).reshape(n_ssm, n, N_GROUPS, STATE_DIM)

    def bufs(parts, n):
        return jnp.transpose(jnp.stack(parts).reshape(n_conv, HIST, n, D_MODEL), (0, 2, 1, 3))

    return (xp, xs.reshape(dec, 1, D_MODEL), states(pre, batch), states(pim, batch), bufs(pbuf, batch),
            states(sre, dec), states(sim, dec), bufs(sbuf, dec))
```

```python
import functools

import jax
import jax.numpy as jnp
from jax import lax
from jax.experimental import pallas as pl
from jax.experimental.pallas import tpu as pltpu

D_MODEL = 1024
DEPTH = 4
GROUP_SIZE = 16
N_GROUPS = D_MODEL // GROUP_SIZE
STATE_DIM = 64
N_STATE = N_GROUPS * STATE_DIM
CONV_WIDTH = 3
HIST = CONV_WIDTH - 1
D_FF = 4 * D_MODEL
RMS_EPS = 1e-6

LANES = 128
MXU_K = 256
N_SLABS = D_MODEL // MXU_K
SLAB_STATE = N_STATE // N_SLABS
CHUNK = MXU_K // GROUP_SIZE
W_CHUNK = 512
VMEM_LIMIT = 58 * 1024 * 1024
ROW_TILE = 512
S5_TILE = 1024
GLU_ROWS = 256
PIPE_COLS = 2


def _rms_norm(x, g):
    ms = jnp.mean(x * x, axis=-1, keepdims=True)
    return x * lax.rsqrt(ms + RMS_EPS) * g


def _const_spec(shape, index=None):
    index = (0,) * len(shape) if index is None else index
    return pl.BlockSpec(shape, lambda i: index, pipeline_mode=pl.Buffered(1))


def _rows_spec(m, first_step, n_tiles):
    return pl.BlockSpec((m, D_MODEL), lambda i: (jnp.clip(i - first_step, 0, n_tiles - 1), 0))


def _natural_spec(batch, steps, first_step, n_tiles):
    return pl.BlockSpec((batch, steps, D_MODEL),
                        lambda i: (0, jnp.clip(i - first_step, 0, n_tiles - 1), 0))


def _col_chunk_spec(layer, n_rows, n_chunks):
    return pl.BlockSpec((None, n_rows, W_CHUNK), lambda i: (layer, 0, jnp.minimum(i, n_chunks - 1)))


def _row_chunk_spec(layer, n_cols, n_chunks):
    return pl.BlockSpec((None, W_CHUNK, n_cols), lambda i: (layer, jnp.minimum(i, n_chunks - 1), 0))


def _to_time_major(x_ref, slab_ref):
    batch, steps, _ = x_ref.shape
    for c in range(D_MODEL // LANES):
        for b in range(batch):
            slab_ref[c, pl.ds(b, steps, stride=batch), :] = x_ref[b, :, c * LANES:(c + 1) * LANES]
    return jnp.concatenate([slab_ref[c] for c in range(D_MODEL // LANES)], axis=1)


def _from_time_major(y, o_ref, slab_ref):
    batch, steps, _ = o_ref.shape
    for c in range(D_MODEL // LANES):
        slab_ref[c] = y[:, c * LANES:(c + 1) * LANES]
        for b in range(batch):
            o_ref[b, :, c * LANES:(c + 1) * LANES] = slab_ref[c, pl.ds(b, steps, stride=batch), :]


def _glu_tail(x, yb, wglu_ref):
    n_half = D_MODEL // W_CHUNK
    outs = []
    for c in range(n_half):
        z_lin = jnp.dot(yb, wglu_ref[c], preferred_element_type=jnp.float32)
        z_gate = jnp.dot(yb, wglu_ref[n_half + c], preferred_element_type=jnp.float32)
        outs.append(z_lin * jax.nn.sigmoid(z_gate))
    return x + jnp.concatenate(outs, axis=1)


def _s5_step_rows(x, hre_ref, him_ref, g_ref, are_ref, aim_ref, wb_ref, wc_ref, d_ref, wglu_ref,
                  bu_ref, hb_ref, y_ref):
    u = _rms_norm(x, g_ref[...])
    ub = u.astype(jnp.bfloat16)
    for s in range(N_SLABS):
        ch = slice(s * MXU_K, (s + 1) * MXU_K)
        bu_ref[...] = jnp.dot(ub[:, ch], wb_ref[s], preferred_element_type=jnp.float32)
        for l0 in range(0, SLAB_STATE, LANES):
            re_l = slice(l0, l0 + LANES)
            im_l = slice(SLAB_STATE + l0, SLAB_STATE + l0 + LANES)
            st_l = slice(s * SLAB_STATE + l0, s * SLAB_STATE + l0 + LANES)
            a_r, a_i = are_ref[:, st_l], aim_ref[:, st_l]
            h_r, h_i = hre_ref[:, st_l], him_ref[:, st_l]
            n_r = a_r * h_r - a_i * h_i + bu_ref[:, re_l]
            n_i = a_r * h_i + a_i * h_r + bu_ref[:, im_l]
            hre_ref[:, st_l] = n_r
            him_ref[:, st_l] = n_i
            hb_ref[:, re_l] = n_r.astype(jnp.bfloat16)
            hb_ref[:, im_l] = n_i.astype(jnp.bfloat16)
        ys = jnp.dot(hb_ref[...], wc_ref[s], preferred_element_type=jnp.float32)
        ys = ys + d_ref[:, ch] * u[:, ch]
        y_ref[:, ch] = jax.nn.gelu(ys).astype(jnp.bfloat16)
    return _glu_tail(x, y_ref[...], wglu_ref)


def _s5_sample_kernel(xs_ref, h0re_ref, h0im_ref, g_ref, are_ref, aim_ref, wb_ref, wc_ref, d_ref, wglu_f32_ref,
                      os_ref, sre_ref, sim_ref, wglu_ref, hre_ref, him_ref, bu_ref, hb_ref, y_ref, *, n_cast):
    i = pl.program_id(0)

    @pl.when(i < n_cast)
    def _():
        wglu_ref[i] = wglu_f32_ref[...].astype(jnp.bfloat16)

    @pl.when(i == n_cast)
    def _():
        dec = xs_ref.shape[0]
        hre_ref[...] = h0re_ref[...].reshape(dec, N_STATE)
        him_ref[...] = h0im_ref[...].reshape(dec, N_STATE)
        os_ref[...] = _s5_step_rows(xs_ref[...], hre_ref, him_ref, g_ref, are_ref, aim_ref, wb_ref, wc_ref,
                                    d_ref, wglu_ref, bu_ref, hb_ref, y_ref)
        sre_ref[...] = hre_ref[...].reshape(dec, N_GROUPS, STATE_DIM)
        sim_ref[...] = him_ref[...].reshape(dec, N_GROUPS, STATE_DIM)


def _s5_sample(xs, h0_re, h0_im, gain, step_params, d_skip, w_glu, *, layer, mixer):
    a_re, a_im, wb, wc = step_params
    dec = xs.shape[0]
    n_cast = (2 * D_MODEL) // W_CHUNK
    return pl.pallas_call(
        functools.partial(_s5_sample_kernel, n_cast=n_cast),
        grid=(n_cast + 1,),
        in_specs=[
            _const_spec((dec, D_MODEL) if xs.ndim == 2 else (dec, None, D_MODEL)),
            _const_spec((None, dec, N_GROUPS, STATE_DIM), (mixer, 0, 0, 0)),
            _const_spec((None, dec, N_GROUPS, STATE_DIM), (mixer, 0, 0, 0)),
            _const_spec((None, 1, D_MODEL), (layer, 0, 0)),
            _const_spec((None, 1, N_STATE), (mixer, 0, 0)), _const_spec((None, 1, N_STATE), (mixer, 0, 0)),
            _const_spec((None, N_SLABS, MXU_K, 2 * SLAB_STATE), (mixer, 0, 0, 0)),
            _const_spec((None, N_SLABS, 2 * SLAB_STATE, MXU_K), (mixer, 0, 0, 0)),
            _const_spec((None, 1, D_MODEL), (mixer, 0, 0)),
            _col_chunk_spec(mixer, D_MODEL, n_cast),
        ],
        out_specs=[
            pl.BlockSpec((dec, D_MODEL), lambda i: (0, 0)),
            pl.BlockSpec((dec, N_GROUPS, STATE_DIM), lambda i: (0, 0, 0)),
            pl.BlockSpec((dec, N_GROUPS, STATE_DIM), lambda i: (0, 0, 0)),
        ],
        out_shape=[
            jax.ShapeDtypeStruct((dec, D_MODEL), jnp.float32),
            jax.ShapeDtypeStruct((dec, N_GROUPS, STATE_DIM), jnp.float32),
            jax.ShapeDtypeStruct((dec, N_GROUPS, STATE_DIM), jnp.float32),
        ],
        scratch_shapes=[
            pltpu.VMEM((n_cast, D_MODEL, W_CHUNK), jnp.bfloat16),
            pltpu.VMEM((dec, N_STATE), jnp.float32),
            pltpu.VMEM((dec, N_STATE), jnp.float32),
            pltpu.VMEM((dec, 2 * SLAB_STATE), jnp.float32),
            pltpu.VMEM((dec, 2 * SLAB_STATE), jnp.bfloat16),
            pltpu.VMEM((dec, D_MODEL), jnp.bfloat16),
        ],
        compiler_params=pltpu.CompilerParams(dimension_semantics=("arbitrary",), vmem_limit_bytes=VMEM_LIMIT),
        name="s5_sample",
    )(xs, h0_re, h0_im, gain, a_re, a_im, wb, wc, d_skip, w_glu)


def _block_transpose(sets, lane_block):
    sets = [list(vs) for vs in sets]
    n = len(sets[0])
    d = 1
    while d < n:
        upper = (lane_block & d) != 0
        for vs in sets:
            for i in range(n):
                if i & d == 0:
                    a, b = vs[i], vs[i + d]
                    vs[i] = jnp.where(upper, pltpu.roll(b, GROUP_SIZE * d, 1), a)
                    vs[i + d] = jnp.where(upper, b, pltpu.roll(a, LANES - GROUP_SIZE * d, 1))
        d *= 2
    return sets


def _s5_chunk_rows(x, hst_ref, g_ref, ws_ref, wt_ref, wy_ref, a1_ref, a2_ref, y_ref,
                   lhs_ref, sc_ref, hp_ref, yg_ref, *, batch):
    m = x.shape[0]
    n_chunks = m // (CHUNK * batch)
    blocks = LANES // GROUP_SIZE
    n_cols = D_MODEL // LANES
    n_halves = CHUNK // blocks
    pass_chunks = min(n_chunks, 4)
    pass_rows = pass_chunks * batch
    u = _rms_norm(x, g_ref[...])
    lane_block = lax.broadcasted_iota(jnp.int32, (pass_rows, LANES), 1) // GROUP_SIZE

    def step_rows(c, s):
        r0 = (c * CHUNK + s) * batch
        return slice(r0, r0 + batch)

    def gather(qs):
        for c0 in range(0, n_chunks, pass_chunks):
            prow = slice(c0 * batch, c0 * batch + pass_rows)
            sets = [[jnp.concatenate([u[step_rows(c0 + c, k * blocks + s), q * LANES:(q + 1) * LANES]
                                      for c in range(pass_chunks)], axis=0).astype(jnp.bfloat16)
                     for s in range(blocks)]
                    for q in qs for k in range(n_halves)]
            w = _block_transpose(sets, lane_block)
            for iq, q in enumerate(qs):
                for j in range(blocks):
                    lhs = jnp.concatenate([w[iq * n_halves + k][j] for k in range(n_halves)], axis=1)
                    lhs_ref[q * blocks + j, prow, :] = lhs

    def project(gs):
        for g in gs:
            sc_ref[g] = jnp.dot(lhs_ref[g], ws_ref[g], preferred_element_type=jnp.float32)
        for g in gs:
            recur(g)
        for g in gs:
            yg_ref[g] = (jnp.dot(lhs_ref[g], wt_ref[g], preferred_element_type=jnp.float32)
                         + jnp.dot(hp_ref[g], wy_ref[g], preferred_element_type=jnp.float32)
                         ).astype(jnp.bfloat16)

    def recur(g):
        a1 = a1_ref[g:g + 1, :]
        a2 = a2_ref[g:g + 1, :]
        sc = sc_ref[g]
        sc_sw = pltpu.roll(sc, STATE_DIM, 1)
        h = hst_ref[g]
        hs = pltpu.roll(h, STATE_DIM, 1)
        h_prev = []
        for c in range(n_chunks):
            h_prev.append(h)
            rows_c = slice(c * batch, (c + 1) * batch)
            h, hs = a1 * h + a2 * hs + sc[rows_c], a1 * hs - a2 * h + sc_sw[rows_c]
        hst_ref[g] = h
        hp_ref[g] = jnp.concatenate(h_prev, axis=0).astype(jnp.bfloat16)

    def scatter(qs):
        for c0 in range(0, n_chunks, pass_chunks):
            prow = slice(c0 * batch, c0 * batch + pass_rows)
            sets = [[yg_ref[q * blocks + j, prow, k * LANES:(k + 1) * LANES] for j in range(blocks)]
                    for q in qs for k in range(n_halves)]
            z = _block_transpose(sets, lane_block)
            for iq, q in enumerate(qs):
                for k in range(n_halves):
                    for s in range(blocks):
                        zs = z[iq * n_halves + k][s].astype(jnp.float32)
                        for c in range(pass_chunks):
                            y_ref[step_rows(c0 + c, k * blocks + s), q * LANES:(q + 1) * LANES] = (
                                zs[c * batch:(c + 1) * batch])

    col_blocks = [list(range(q0, q0 + PIPE_COLS)) for q0 in range(0, n_cols, PIPE_COLS)]
    grp_blocks = [[q * blocks + j for q in qs for j in range(blocks)] for qs in col_blocks]
    n_blk = len(col_blocks)
    for step in range(n_blk + 2):
        if step < n_blk:
            gather(col_blocks[step])
        if 1 <= step <= n_blk:
            project(grp_blocks[step - 1])
        if step >= 2:
            scatter(col_blocks[step - 2])


def _s5_core_kernel(xp_ref, g_ref, ws_ref, kc_ref, wy_ref, a1_ref, a2_ref,
                    y_ref, pre_ref, pim_ref, wt_ref, hst_ref, lhs_ref, sc_ref, hp_ref, yg_ref,
                    *maybe_slab_ref, n_tiles, batch):
    i = pl.program_id(0)

    @pl.when(i == 0)
    def _():
        hst_ref[...] = jnp.zeros_like(hst_ref)
        lane = lax.broadcasted_iota(jnp.int32, (GROUP_SIZE, MXU_K), 1)

        def build(g, carry):
            kc = kc_ref[g]
            for s in range(CHUNK):
                blk = kc if s == 0 else jnp.where(lane >= s * GROUP_SIZE, pltpu.roll(kc, s * GROUP_SIZE, 1), 0.0)
                wt_ref[g, s * GROUP_SIZE:(s + 1) * GROUP_SIZE, :] = blk.astype(jnp.bfloat16)
            return carry

        lax.fori_loop(0, N_GROUPS, build, 0)

    x = _to_time_major(xp_ref, maybe_slab_ref[0]) if maybe_slab_ref else xp_ref[...]
    _s5_chunk_rows(x, hst_ref, g_ref, ws_ref, wt_ref, wy_ref, a1_ref, a2_ref,
                   y_ref, lhs_ref, sc_ref, hp_ref, yg_ref, batch=batch)

    @pl.when(i == n_tiles - 1)
    def _():
        low = lax.broadcasted_iota(jnp.int32, (batch, LANES), 1) < STATE_DIM
        re_parts, im_parts = [], []
        for g in range(0, N_GROUPS, 2):
            h0, h1 = hst_ref[g], hst_ref[g + 1]
            re_parts.append(jnp.where(low, h0, pltpu.roll(h1, STATE_DIM, 1)))
            im_parts.append(jnp.where(low, pltpu.roll(h0, STATE_DIM, 1), h1))
        pre_ref[...] = jnp.concatenate(re_parts, axis=1).reshape(batch, N_GROUPS, STATE_DIM)
        pim_ref[...] = jnp.concatenate(im_parts, axis=1).reshape(batch, N_GROUPS, STATE_DIM)


def _s5_core(xp, gain, chunk_params, *, layer, mixer, batch, seq):
    natural = xp.ndim == 3
    ws, kc, wy, a1, a2 = chunk_params
    m = S5_TILE
    n_tiles = (seq * batch) // m

    def xp_spec(first_step):
        return (_natural_spec(batch, m // batch, first_step, n_tiles) if natural
                else _rows_spec(m, first_step, n_tiles))

    slab = [pltpu.VMEM((D_MODEL // LANES, m, LANES), jnp.float32)] if natural else []
    scratch = [
        pltpu.VMEM((N_GROUPS, MXU_K, MXU_K), jnp.bfloat16),
        pltpu.VMEM((N_GROUPS, batch, 2 * STATE_DIM), jnp.float32),
        pltpu.VMEM((N_GROUPS, m // CHUNK, MXU_K), jnp.bfloat16),
        pltpu.VMEM((N_GROUPS, m // CHUNK, 2 * STATE_DIM), jnp.float32),
        pltpu.VMEM((N_GROUPS, m // CHUNK, 2 * STATE_DIM), jnp.bfloat16),
        pltpu.VMEM((N_GROUPS, m // CHUNK, MXU_K), jnp.bfloat16),
    ]
    y, pre, pim = pl.pallas_call(
        functools.partial(_s5_core_kernel, n_tiles=n_tiles, batch=batch),
        grid=(n_tiles,),
        in_specs=[
            xp_spec(0),
            _const_spec((None, 1, D_MODEL), (layer, 0, 0)),
            _const_spec((None, N_GROUPS, MXU_K, 2 * STATE_DIM), (mixer, 0, 0, 0)),
            _const_spec((None, N_GROUPS, GROUP_SIZE, MXU_K), (mixer, 0, 0, 0)),
            _const_spec((None, N_GROUPS, 2 * STATE_DIM, MXU_K), (mixer, 0, 0, 0)),
            _const_spec((None, N_GROUPS, 2 * STATE_DIM), (mixer, 0, 0)),
            _const_spec((None, N_GROUPS, 2 * STATE_DIM), (mixer, 0, 0)),
        ],
        out_specs=[
            _rows_spec(m, 0, n_tiles),
            pl.BlockSpec((batch, N_GROUPS, STATE_DIM), lambda i: (0, 0, 0)),
            pl.BlockSpec((batch, N_GROUPS, STATE_DIM), lambda i: (0, 0, 0)),
        ],
        out_shape=[
            jax.ShapeDtypeStruct((seq * batch, D_MODEL), jnp.float32),
            jax.ShapeDtypeStruct((batch, N_GROUPS, STATE_DIM), jnp.float32),
            jax.ShapeDtypeStruct((batch, N_GROUPS, STATE_DIM), jnp.float32),
        ],
        scratch_shapes=scratch + slab,
        compiler_params=pltpu.CompilerParams(dimension_semantics=("arbitrary",), vmem_limit_bytes=VMEM_LIMIT),
        name="s5_core",
    )(xp, gain, ws, kc, wy, a1, a2)
    return y, pre, pim


def _s5_finish_rows(x, y_ref, g_ref, d_ref, wglu_ref):
    parts = []
    for r0 in range(0, x.shape[0], GLU_ROWS):
        rows = slice(r0, r0 + GLU_ROWS)
        ys = y_ref[rows, :] + d_ref[...] * _rms_norm(x[rows], g_ref[...])
        parts.append(_glu_tail(x[rows], jax.nn.gelu(ys).astype(jnp.bfloat16), wglu_ref))
    return jnp.concatenate(parts, axis=0)


def _conv_rows(x, g_ref, win_ref, cw_ref, wout_ref, cvx_ref, *, rows_per_step):
    br = rows_per_step
    m = x.shape[0]
    hist = HIST * br
    n_d = D_MODEL // W_CHUNK
    ub = _rms_norm(x, g_ref[...]).astype(jnp.bfloat16)
    bcv = [jnp.dot(ub, win_ref[c], preferred_element_type=jnp.float32) for c in range(3 * n_d)]
    outs = 0.0
    for c in range(n_d):
        cols = slice(c * W_CHUNK, (c + 1) * W_CHUNK)
        cv = bcv[n_d + c] * bcv[2 * n_d + c]
        cvx_ref[hist:hist + m, cols] = cv
        y = cw_ref[CONV_WIDTH - 1:CONV_WIDTH, cols] * cv
        for k in range(CONV_WIDTH - 1):
            y = y + cw_ref[k:k + 1, cols] * cvx_ref[k * br:k * br + m, cols]
        gated = (bcv[c] * y).astype(jnp.bfloat16)
        outs = outs + jnp.dot(gated, wout_ref[c], preferred_element_type=jnp.float32)
    cvx_ref[0:hist, :] = cvx_ref[m:m + hist, :]
    return x + outs


def _conv_kernel(xp_ref, xs_ref, bufs_ref, g_ref, win_f32_ref, cw_ref, wout_f32_ref,
                 op_ref, os_ref, nbufp_ref, nbufs_ref, win_ref, wout_ref, cvx_ref, *, n_cast, n_tiles, batch):
    i = pl.program_id(0)
    dec = xs_ref.shape[0]
    body = functools.partial(_conv_rows, g_ref=g_ref, win_ref=win_ref, cw_ref=cw_ref, wout_ref=wout_ref,
                             cvx_ref=cvx_ref)

    @pl.when(i < n_cast)
    def _():
        win_ref[i] = win_f32_ref[...].astype(jnp.bfloat16)

    @pl.when(i < wout_ref.shape[0])
    def _():
        wout_ref[i] = wout_f32_ref[...].astype(jnp.bfloat16)

    @pl.when(i == n_cast)
    def _():
        cvx_ref[0:HIST * batch, :] = jnp.zeros((HIST * batch, D_MODEL), jnp.float32)

    @pl.when(jnp.logical_and(i >= n_cast, i < n_cast + n_tiles))
    def _():
        op_ref[...] = body(xp_ref[...], rows_per_step=batch)

    @pl.when(i == n_cast + n_tiles - 1)
    def _():
        nbufp_ref[...] = cvx_ref[0:HIST * batch, :]

    @pl.when(i == n_cast + n_tiles)
    def _():
        for k in range(HIST):
            cvx_ref[k * dec:(k + 1) * dec, :] = bufs_ref[:, k, :]
        os_ref[...] = body(xs_ref[...], rows_per_step=dec)
        for k in range(HIST):
            nbufs_ref[:, k, :] = cvx_ref[k * dec:(k + 1) * dec, :]


def _conv_mixer(xp, xs, buf_s, gain, w_in, conv_w, w_out, *, layer, mixer, batch, seq):
    dec = xs.shape[0]
    m = ROW_TILE
    n_tiles = (seq * batch) // m
    n_cast = (3 * D_MODEL) // W_CHUNK
    n_out = D_MODEL // W_CHUNK
    kern = functools.partial(_conv_kernel, n_cast=n_cast, n_tiles=n_tiles, batch=batch)
    return pl.pallas_call(
        kern,
        grid=(n_cast + n_tiles + 1,),
        in_specs=[
            _rows_spec(m, n_cast, n_tiles),
            _const_spec((dec, D_MODEL)),
            _const_spec((None, dec, HIST, D_MODEL), (mixer, 0, 0, 0)),
            _const_spec((None, 1, D_MODEL), (layer, 0, 0)),
            _col_chunk_spec(mixer, D_MODEL, n_cast),
            _const_spec((None, CONV_WIDTH, D_MODEL), (mixer, 0, 0)),
            _row_chunk_spec(mixer, D_MODEL, n_out),
        ],
        out_specs=[
            _rows_spec(m, n_cast, n_tiles),
            pl.BlockSpec((dec, D_MODEL), lambda i: (0, 0)),
            pl.BlockSpec((HIST * batch, D_MODEL), lambda i: (0, 0)),
            pl.BlockSpec((dec, HIST, D_MODEL), lambda i: (0, 0, 0)),
        ],
        out_shape=[
            jax.ShapeDtypeStruct((seq * batch, D_MODEL), jnp.float32),
            jax.ShapeDtypeStruct((dec, D_MODEL), jnp.float32),
            jax.ShapeDtypeStruct((HIST * batch, D_MODEL), jnp.float32),
            jax.ShapeDtypeStruct((dec, HIST, D_MODEL), jnp.float32),
        ],
        scratch_shapes=[
            pltpu.VMEM((n_cast, D_MODEL, W_CHUNK), jnp.bfloat16),
            pltpu.VMEM((n_out, W_CHUNK, D_MODEL), jnp.bfloat16),
            pltpu.VMEM((max(HIST * batch + m, (HIST + 1) * dec), D_MODEL), jnp.float32),
        ],
        compiler_params=pltpu.CompilerParams(dimension_semantics=("arbitrary",), vmem_limit_bytes=VMEM_LIMIT),
        name="conv_mixer",
    )(xp, xs, buf_s, gain, w_in, conv_w, w_out)


def _mlp_chunk(ub, up_ref, down_ref, c):
    h = jnp.dot(ub, up_ref[c], preferred_element_type=jnp.float32)
    h = jnp.maximum(h, 0.0)
    h = (h * h).astype(jnp.bfloat16)
    return jnp.dot(h, down_ref[c], preferred_element_type=jnp.float32)


def _mlp_rows(x, g_ref, up_ref, down_ref, gf_ref, *, final_norm):
    ub = _rms_norm(x, g_ref[...]).astype(jnp.bfloat16)
    acc = x
    for c in range(D_FF // W_CHUNK):
        acc = acc + _mlp_chunk(ub, up_ref, down_ref, c)
    if final_norm:
        acc = _rms_norm(acc, gf_ref[...])
    return acc


def _mlp_kernel(xp_ref, xs_ref, g_ref, up_f32_ref, down_f32_ref, gf_ref, op_ref, os_ref,
                up_ref, down_ref, ub0_ref, acc0_ref, *maybe_slab_ref, n_cast, n_tiles, final_norm):
    i = pl.program_id(0)
    body = functools.partial(_mlp_rows, g_ref=g_ref, up_ref=up_ref, down_ref=down_ref, gf_ref=gf_ref,
                             final_norm=final_norm)

    def write_prompt(y):
        if maybe_slab_ref:
            _from_time_major(y, op_ref, maybe_slab_ref[0])
        else:
            op_ref[...] = y

    @pl.when(i == 0)
    def _():
        x = xp_ref[...]
        ub0_ref[...] = _rms_norm(x, g_ref[...]).astype(jnp.bfloat16)
        acc0_ref[...] = x

    @pl.when(i < n_cast)
    def _():
        up_ref[i] = up_f32_ref[...].astype(jnp.bfloat16)
        down_ref[i] = down_f32_ref[...].astype(jnp.bfloat16)
        acc0_ref[...] += _mlp_chunk(ub0_ref[...], up_ref, down_ref, i)

    @pl.when(i == n_cast - 1)
    def _():
        y = acc0_ref[...]
        write_prompt(_rms_norm(y, gf_ref[...]) if final_norm else y)

    @pl.when(jnp.logical_and(i >= n_cast, i < n_cast + n_tiles - 1))
    def _():
        write_prompt(body(xp_ref[...]))

    @pl.when(i == n_cast + n_tiles - 1)
    def _():
        os_ref[...] = body(xs_ref[...])


def _mlp(xp, xs, gain, w_up, w_down, gain_final, *, layer, batch, seq, final_norm):
    dec = xs.shape[0]
    m = ROW_TILE
    steps = m // batch
    n_tiles = (seq * batch) // m
    n_cast = D_FF // W_CHUNK
    scratch = [
        pltpu.VMEM((n_cast, D_MODEL, W_CHUNK), jnp.bfloat16),
        pltpu.VMEM((n_cast, W_CHUNK, D_MODEL), jnp.bfloat16),
        pltpu.VMEM((m, D_MODEL), jnp.bfloat16),
        pltpu.VMEM((m, D_MODEL), jnp.float32),
    ]
    first = n_cast - 1
    if final_norm:
        scratch.append(pltpu.VMEM((D_MODEL // LANES, m, LANES), jnp.float32))
        op_spec = _natural_spec(batch, steps, first, n_tiles)
        op_shape = jax.ShapeDtypeStruct((batch, seq, D_MODEL), jnp.float32)
    else:
        op_spec = _rows_spec(m, first, n_tiles)
        op_shape = jax.ShapeDtypeStruct((seq * batch, D_MODEL), jnp.float32)
    kern = functools.partial(_mlp_kernel, n_cast=n_cast, n_tiles=n_tiles, final_norm=final_norm)
    return pl.pallas_call(
        kern,
        grid=(n_cast + n_tiles,),
        in_specs=[
            _rows_spec(m, first, n_tiles),
            _const_spec((dec, D_MODEL)),
            _const_spec((None, 1, D_MODEL), (layer, 0, 0)),
            _col_chunk_spec(layer, D_MODEL, n_cast),
            _row_chunk_spec(layer, D_MODEL, n_cast),
            _const_spec((1, D_MODEL)),
        ],
        out_specs=[op_spec, (pl.BlockSpec((dec, None, D_MODEL), lambda i: (0, 0, 0)) if final_norm
                             else pl.BlockSpec((dec, D_MODEL), lambda i: (0, 0)))],
        out_shape=[op_shape, jax.ShapeDtypeStruct((dec, 1, D_MODEL) if final_norm else (dec, D_MODEL), jnp.float32)],
        scratch_shapes=scratch,
        compiler_params=pltpu.CompilerParams(dimension_semantics=("arbitrary",), vmem_limit_bytes=VMEM_LIMIT),
        name="mlp",
    )(xp, xs, gain, w_up, w_down, gain_final)


def _glu_mlp_kernel(xp_ref, y_ref, xs_ref, gmix_ref, d_ref, wglu_f32_ref, g_ref, up_f32_ref, down_f32_ref,
                    op_ref, os_ref, wglu_ref, up_ref, down_ref, ub0_ref, acc0_ref, *maybe_slab_ref,
                    n_pre, n_cast, n_tiles):
    i = pl.program_id(0)
    body = functools.partial(_mlp_rows, g_ref=g_ref, up_ref=up_ref, down_ref=down_ref, gf_ref=None,
                             final_norm=False)

    def mixed():
        x = _to_time_major(xp_ref, maybe_slab_ref[0]) if maybe_slab_ref else xp_ref[...]
        return _s5_finish_rows(x, y_ref, gmix_ref, d_ref, wglu_ref)

    @pl.when(i < n_pre)
    def _():
        wglu_ref[i] = wglu_f32_ref[...].astype(jnp.bfloat16)

    @pl.when(i == n_pre)
    def _():
        x = mixed()
        ub0_ref[...] = _rms_norm(x, g_ref[...]).astype(jnp.bfloat16)
        acc0_ref[...] = x

    @pl.when(jnp.logical_and(i >= n_pre, i < n_pre + n_cast))
    def _():
        c = i - n_pre
        up_ref[c] = up_f32_ref[...].astype(jnp.bfloat16)
        down_ref[c] = down_f32_ref[...].astype(jnp.bfloat16)
        acc0_ref[...] += _mlp_chunk(ub0_ref[...], up_ref, down_ref, c)

    @pl.when(i == n_pre + n_cast - 1)
    def _():
        op_ref[...] = acc0_ref[...]

    @pl.when(jnp.logical_and(i >= n_pre + n_cast, i < n_pre + n_cast + n_tiles - 1))
    def _():
        op_ref[...] = body(mixed())

    @pl.when(i == n_pre + n_cast + n_tiles - 1)
    def _():
        os_ref[...] = body(xs_ref[...])


def _glu_mlp(xp, y, xs, gain_mix, d_skip, w_glu, gain, w_up, w_down, *, layer, mixer, batch, seq):
    natural = xp.ndim == 3
    dec = xs.shape[0]
    m = ROW_TILE
    n_tiles = (seq * batch) // m
    n_pre = (2 * D_MODEL) // W_CHUNK
    n_cast = D_FF // W_CHUNK
    first = n_pre + n_cast - 1
    xp_spec = (_natural_spec(batch, m // batch, first, n_tiles) if natural else _rows_spec(m, first, n_tiles))
    slab = [pltpu.VMEM((D_MODEL // LANES, m, LANES), jnp.float32)] if natural else []
    return pl.pallas_call(
        functools.partial(_glu_mlp_kernel, n_pre=n_pre, n_cast=n_cast, n_tiles=n_tiles),
        grid=(n_pre + n_cast + n_tiles,),
        in_specs=[
            xp_spec,
            _rows_spec(m, first, n_tiles),
            _const_spec((dec, D_MODEL)),
            _const_spec((None, 1, D_MODEL), (layer, 0, 0)),
            _const_spec((None, 1, D_MODEL), (mixer, 0, 0)),
            _col_chunk_spec(mixer, D_MODEL, n_pre),
            _const_spec((None, 1, D_MODEL), (layer, 0, 0)),
            pl.BlockSpec((None, D_MODEL, W_CHUNK), lambda i: (layer, 0, jnp.clip(i - n_pre, 0, n_cast - 1))),
            pl.BlockSpec((None, W_CHUNK, D_MODEL), lambda i: (layer, jnp.clip(i - n_pre, 0, n_cast - 1), 0)),
        ],
        out_specs=[_rows_spec(m, first, n_tiles), pl.BlockSpec((dec, D_MODEL), lambda i: (0, 0))],
        out_shape=[jax.ShapeDtypeStruct((seq * batch, D_MODEL), jnp.float32),
                   jax.ShapeDtypeStruct((dec, D_MODEL), jnp.float32)],
        scratch_shapes=[
            pltpu.VMEM((n_pre, D_MODEL, W_CHUNK), jnp.bfloat16),
            pltpu.VMEM((n_cast, D_MODEL, W_CHUNK), jnp.bfloat16),
            pltpu.VMEM((n_cast, W_CHUNK, D_MODEL), jnp.bfloat16),
            pltpu.VMEM((m, D_MODEL), jnp.bfloat16),
            pltpu.VMEM((m, D_MODEL), jnp.float32),
        ] + slab,
        compiler_params=pltpu.CompilerParams(dimension_semantics=("arbitrary",), vmem_limit_bytes=VMEM_LIMIT),
        name="glu_mlp",
    )(xp, y, xs, gain_mix, d_skip, w_glu, gain, w_up, w_down)


def _s5_discretise(a_re, a_im, log_dt, b_re, b_im):
    dt = jnp.exp(log_dt)[:, None]
    k = jnp.arange(CHUNK + 1, dtype=jnp.float32)[:, None, None]
    mag = jnp.exp(k * (a_re * dt))
    pw_re = mag * jnp.cos(k * (a_im * dt))
    pw_im = mag * jnp.sin(k * (a_im * dt))
    ab_re, ab_im = pw_re[1], pw_im[1]
    den = a_re * a_re + a_im * a_im
    q_re = ((ab_re - 1.0) * a_re + ab_im * a_im) / den
    q_im = (ab_im * a_re - (ab_re - 1.0) * a_im) / den
    bb_re = q_re[..., None] * b_re - q_im[..., None] * b_im
    bb_im = q_re[..., None] * b_im + q_im[..., None] * b_re
    return pw_re, pw_im, bb_re, bb_im


def _s5_step_params(pw_re, pw_im, bb_re, bb_im, c_re, c_im):
    gps = MXU_K // GROUP_SIZE
    eye = jnp.eye(gps, dtype=jnp.float32)

    def pack_b(w):
        w = w.reshape(N_SLABS, gps, STATE_DIM, GROUP_SIZE)
        return jnp.einsum('sgph,gk->sghkp', w, eye).reshape(N_SLABS, MXU_K, SLAB_STATE)

    def pack_c(w):
        w = w.reshape(N_SLABS, gps, GROUP_SIZE, STATE_DIM)
        return jnp.einsum('sghp,gk->sgpkh', w, eye).reshape(N_SLABS, SLAB_STATE, MXU_K)

    wb = jnp.concatenate([pack_b(bb_re), pack_b(bb_im)], axis=2).astype(jnp.bfloat16)
    wc = jnp.concatenate([pack_c(c_re), pack_c(-c_im)], axis=1).astype(jnp.bfloat16)
    return pw_re[1].reshape(1, N_STATE), pw_im[1].reshape(1, N_STATE), wb, wc


def _s5_chunk_params(pw_re, pw_im, bb_re, bb_im, c_re, c_im):
    ps_re = jnp.transpose(pw_re[CHUNK - 1::-1], (1, 0, 2))[:, :, None, :]
    ps_im = jnp.transpose(pw_im[CHUNK - 1::-1], (1, 0, 2))[:, :, None, :]
    bt_re = jnp.transpose(bb_re, (0, 2, 1))[:, None]
    bt_im = jnp.transpose(bb_im, (0, 2, 1))[:, None]
    ws = jnp.concatenate([ps_re * bt_re - ps_im * bt_im, ps_re * bt_im + ps_im * bt_re],
                         axis=-1).reshape(N_GROUPS, MXU_K, 2 * STATE_DIM)
    pt_re = jnp.transpose(pw_re, (1, 2, 0))
    pt_im = jnp.transpose(pw_im, (1, 2, 0))
    ct_re = jnp.transpose(c_re, (0, 2, 1))[:, :, None, :]
    ct_im = jnp.transpose(c_im, (0, 2, 1))[:, :, None, :]

    def ca(t0):
        p_r = pt_re[:, :, t0:t0 + CHUNK, None]
        p_i = pt_im[:, :, t0:t0 + CHUNK, None]
        return ((ct_re * p_r - ct_im * p_i).reshape(N_GROUPS, STATE_DIM, MXU_K),
                (ct_re * p_i + ct_im * p_r).reshape(N_GROUPS, STATE_DIM, MXU_K))

    ca0_re, ca0_im = ca(0)
    ca1_re, ca1_im = ca(1)
    hi = lax.Precision.HIGHEST
    kc = (jnp.einsum('gph,gpx->ghx', bb_re, ca0_re, precision=hi)
          - jnp.einsum('gph,gpx->ghx', bb_im, ca0_im, precision=hi))
    wy = jnp.concatenate([ca1_re, -ca1_im], axis=1)
    a1 = jnp.concatenate([pw_re[CHUNK], pw_re[CHUNK]], axis=-1)
    a2 = jnp.concatenate([-pw_im[CHUNK], pw_im[CHUNK]], axis=-1)
    return ws.astype(jnp.bfloat16), kc, wy.astype(jnp.bfloat16), a1, a2


def kernel(x_prompt, x_sample, state_ssm_re, state_ssm_im, state_conv, norm_mix, norm_mlp, norm_final,
           ssm_a_re, ssm_a_im, ssm_log_dt, ssm_b_re, ssm_b_im, ssm_c_re, ssm_c_im, ssm_d, ssm_w_glu,
           conv_w_in, conv_w, conv_w_out, mlp_w_up, mlp_w_down):
    batch, seq, _ = x_prompt.shape
    dec = x_sample.shape[0]
    n_ssm = ssm_a_re.shape[0]
    n_conv = conv_w_in.shape[0]

    g_mix = norm_mix.reshape(DEPTH, 1, D_MODEL)
    g_mlp = norm_mlp.reshape(DEPTH, 1, D_MODEL)
    g_final = norm_final.reshape(1, D_MODEL)
    d_skip = ssm_d.reshape(n_ssm, 1, D_MODEL)
    disc = jax.vmap(_s5_discretise)(ssm_a_re, ssm_a_im, ssm_log_dt, ssm_b_re, ssm_b_im)
    step_params = jax.vmap(_s5_step_params)(*disc, ssm_c_re, ssm_c_im)
    chunk_params = jax.vmap(_s5_chunk_params)(*disc, ssm_c_re, ssm_c_im)
    h0_re, h0_im = state_ssm_re, state_ssm_im
    buf_s = state_conv

    xp = x_prompt
    xs = x_sample
    pre, pim, sre, sim, pbuf, sbuf = [], [], [], [], [], []
    for i in range(DEPTH):
        j = i // 2
        if i % 2 == 0:
            y, a, b = _s5_core(xp, g_mix, chunk_params, layer=i, mixer=j, batch=batch, seq=seq)
            xs, c, d = _s5_sample(xs, h0_re, h0_im, g_mix, step_params, d_skip, ssm_w_glu, layer=i, mixer=j)
            pre.append(a), pim.append(b), sre.append(c), sim.append(d)
            xp, xs = _glu_mlp(xp, y, xs, g_mix, d_skip, ssm_w_glu, g_mlp, mlp_w_up, mlp_w_down,
                              layer=i, mixer=j, batch=batch, seq=seq)
        else:
            xp, xs, a, b = _conv_mixer(xp, xs, buf_s, g_mix, conv_w_in, conv_w, conv_w_out,
                                       layer=i, mixer=j, batch=batch, seq=seq)
            pbuf.append(a), sbuf.append(b)
            xp, xs = _mlp(xp, xs, g_mlp, mlp_w_up, mlp_w_down, g_final, layer=i, batch=batch, seq=seq,
                          final_norm=(i == DEPTH - 1))

    def bufs(parts, n):
        return jnp.transpose(jnp.stack(parts).reshape(n_conv, HIST, n, D_MODEL), (0, 2, 1, 3))

    return (xp, xs, jnp.stack(pre), jnp.stack(pim), bufs(pbuf, batch),
            jnp.stack(sre), jnp.stack(sim), jnp.stack(sbuf))
```

```python
import functools

import jax
import jax.numpy as jnp
from jax import lax
from jax.experimental import pallas as pl
from jax.experimental.pallas import tpu as pltpu

D_MODEL = 1024
DEPTH = 4
GROUP_SIZE = 16
N_GROUPS = D_MODEL // GROUP_SIZE
STATE_DIM = 64
N_STATE = N_GROUPS * STATE_DIM
CONV_WIDTH = 3
HIST = CONV_WIDTH - 1
D_FF = 4 * D_MODEL
RMS_EPS = 1e-6

LANES = 128
MXU_K = 256
N_SLABS = D_MODEL // MXU_K
SLAB_STATE = N_STATE // N_SLABS
CHUNK = MXU_K // GROUP_SIZE
W_CHUNK = 512
VMEM_LIMIT = 58 * 1024 * 1024
ROW_TILE = 512
CONV_TILE = 1024
S5_TILE = 1024
GLU_ROWS = 256
PIPE_COLS = 2


def _rms_norm(x, g):
    ms = jnp.mean(x * x, axis=-1, keepdims=True)
    return x * lax.rsqrt(ms + RMS_EPS) * g


def _const_spec(shape, index=None):
    index = (0,) * len(shape) if index is None else index
    return pl.BlockSpec(shape, lambda i: index, pipeline_mode=pl.Buffered(1))


def _rows_spec(m, first_step, n_tiles):
    return pl.BlockSpec((m, D_MODEL), lambda i: (jnp.clip(i - first_step, 0, n_tiles - 1), 0))


def _natural_spec(batch, steps, first_step, n_tiles):
    return pl.BlockSpec((batch, steps, D_MODEL),
                        lambda i: (0, jnp.clip(i - first_step, 0, n_tiles - 1), 0))


def _col_chunk_spec(layer, n_rows, n_chunks):
    return pl.BlockSpec((None, n_rows, W_CHUNK), lambda i: (layer, 0, jnp.minimum(i, n_chunks - 1)))


def _row_chunk_spec(layer, n_cols, n_chunks):
    return pl.BlockSpec((None, W_CHUNK, n_cols), lambda i: (layer, jnp.minimum(i, n_chunks - 1), 0))


def _to_time_major(x_ref, slab_ref):
    batch, steps, _ = x_ref.shape
    for c in range(D_MODEL // LANES):
        for b in range(batch):
            slab_ref[c, pl.ds(b, steps, stride=batch), :] = x_ref[b, :, c * LANES:(c + 1) * LANES]
    return jnp.concatenate([slab_ref[c] for c in range(D_MODEL // LANES)], axis=1)


def _from_time_major(y, o_ref, slab_ref):
    batch, steps, _ = o_ref.shape
    for c in range(D_MODEL // LANES):
        slab_ref[c] = y[:, c * LANES:(c + 1) * LANES]
        for b in range(batch):
            o_ref[b, :, c * LANES:(c + 1) * LANES] = slab_ref[c, pl.ds(b, steps, stride=batch), :]


def _glu_tail(x, yb, wglu_ref):
    n_half = D_MODEL // W_CHUNK
    outs = []
    for c in range(n_half):
        z_lin = jnp.dot(yb, wglu_ref[c], preferred_element_type=jnp.float32)
        z_gate = jnp.dot(yb, wglu_ref[n_half + c], preferred_element_type=jnp.float32)
        outs.append(z_lin * jax.nn.sigmoid(z_gate))
    return x + jnp.concatenate(outs, axis=1)


def _s5_step_rows(x, hre_ref, him_ref, g_ref, are_ref, aim_ref, wb_ref, wc_ref, d_ref, wglu_ref,
                  bu_ref, hb_ref, y_ref):
    u = _rms_norm(x, g_ref[...])
    ub = u.astype(jnp.bfloat16)
    for s in range(N_SLABS):
        ch = slice(s * MXU_K, (s + 1) * MXU_K)
        bu_ref[...] = jnp.dot(ub[:, ch], wb_ref[s], preferred_element_type=jnp.float32)
        for l0 in range(0, SLAB_STATE, LANES):
            re_l = slice(l0, l0 + LANES)
            im_l = slice(SLAB_STATE + l0, SLAB_STATE + l0 + LANES)
            st_l = slice(s * SLAB_STATE + l0, s * SLAB_STATE + l0 + LANES)
            a_r, a_i = are_ref[:, st_l], aim_ref[:, st_l]
            h_r, h_i = hre_ref[:, st_l], him_ref[:, st_l]
            n_r = a_r * h_r - a_i * h_i + bu_ref[:, re_l]
            n_i = a_r * h_i + a_i * h_r + bu_ref[:, im_l]
            hre_ref[:, st_l] = n_r
            him_ref[:, st_l] = n_i
            hb_ref[:, re_l] = n_r.astype(jnp.bfloat16)
            hb_ref[:, im_l] = n_i.astype(jnp.bfloat16)
        ys = jnp.dot(hb_ref[...], wc_ref[s], preferred_element_type=jnp.float32)
        ys = ys + d_ref[:, ch] * u[:, ch]
        y_ref[:, ch] = jax.nn.gelu(ys).astype(jnp.bfloat16)
    return _glu_tail(x, y_ref[...], wglu_ref)


def _s5_sample_kernel(xs_ref, h0re_ref, h0im_ref, g_ref, are_ref, aim_ref, wb_ref, wc_ref, d_ref, wglu_f32_ref,
                      os_ref, sre_ref, sim_ref, wglu_ref, bu_ref, hb_ref, y_ref, *, n_cast):
    i = pl.program_id(0)

    @pl.when(i < n_cast)
    def _():
        wglu_ref[i] = wglu_f32_ref[...].astype(jnp.bfloat16)

    @pl.when(i == n_cast)
    def _():
        sre_ref[...] = h0re_ref[...]
        sim_ref[...] = h0im_ref[...]
        os_ref[...] = _s5_step_rows(xs_ref[...], sre_ref, sim_ref, g_ref, are_ref, aim_ref, wb_ref, wc_ref,
                                    d_ref, wglu_ref, bu_ref, hb_ref, y_ref)


def _s5_sample(xs, h0_re, h0_im, gain, step_params, d_skip, w_glu, *, layer, mixer):
    a_re, a_im, wb, wc = step_params
    dec = xs.shape[0]
    n_cast = (2 * D_MODEL) // W_CHUNK
    return pl.pallas_call(
        functools.partial(_s5_sample_kernel, n_cast=n_cast),
        grid=(n_cast + 1,),
        in_specs=[
            _const_spec((dec, D_MODEL)),
            _const_spec((None, dec, N_STATE), (mixer, 0, 0)), _const_spec((None, dec, N_STATE), (mixer, 0, 0)),
            _const_spec((None, 1, D_MODEL), (layer, 0, 0)),
            _const_spec((None, 1, N_STATE), (mixer, 0, 0)), _const_spec((None, 1, N_STATE), (mixer, 0, 0)),
            _const_spec((None, N_SLABS, MXU_K, 2 * SLAB_STATE), (mixer, 0, 0, 0)),
            _const_spec((None, N_SLABS, 2 * SLAB_STATE, MXU_K), (mixer, 0, 0, 0)),
            _const_spec((None, 1, D_MODEL), (mixer, 0, 0)),
            _col_chunk_spec(mixer, D_MODEL, n_cast),
        ],
        out_specs=[
            pl.BlockSpec((dec, D_MODEL), lambda i: (0, 0)),
            pl.BlockSpec((dec, N_STATE), lambda i: (0, 0)), pl.BlockSpec((dec, N_STATE), lambda i: (0, 0)),
        ],
        out_shape=[
            jax.ShapeDtypeStruct((dec, D_MODEL), jnp.float32),
            jax.ShapeDtypeStruct((dec, N_STATE), jnp.float32), jax.ShapeDtypeStruct((dec, N_STATE), jnp.float32),
        ],
        scratch_shapes=[
            pltpu.VMEM((n_cast, D_MODEL, W_CHUNK), jnp.bfloat16),
            pltpu.VMEM((dec, 2 * SLAB_STATE), jnp.float32),
            pltpu.VMEM((dec, 2 * SLAB_STATE), jnp.bfloat16),
            pltpu.VMEM((dec, D_MODEL), jnp.bfloat16),
        ],
        compiler_params=pltpu.CompilerParams(dimension_semantics=("arbitrary",), vmem_limit_bytes=VMEM_LIMIT),
        name="s5_sample",
    )(xs, h0_re, h0_im, gain, a_re, a_im, wb, wc, d_skip, w_glu)


def _block_transpose(sets, lane_block):
    sets = [list(vs) for vs in sets]
    n = len(sets[0])
    d = 1
    while d < n:
        upper = (lane_block & d) != 0
        for vs in sets:
            for i in range(n):
                if i & d == 0:
                    a, b = vs[i], vs[i + d]
                    vs[i] = jnp.where(upper, pltpu.roll(b, GROUP_SIZE * d, 1), a)
                    vs[i + d] = jnp.where(upper, b, pltpu.roll(a, LANES - GROUP_SIZE * d, 1))
        d *= 2
    return sets


def _s5_chunk_rows(x, hst_ref, g_ref, ws_ref, wt_ref, wy_ref, a1_ref, a2_ref, y_ref,
                   lhs_ref, sc_ref, hp_ref, yg_ref, *, batch):
    m = x.shape[0]
    n_chunks = m // (CHUNK * batch)
    blocks = LANES // GROUP_SIZE
    n_cols = D_MODEL // LANES
    n_halves = CHUNK // blocks
    pass_chunks = min(n_chunks, 4)
    pass_rows = pass_chunks * batch
    u = _rms_norm(x, g_ref[...])
    lane_block = lax.broadcasted_iota(jnp.int32, (pass_rows, LANES), 1) // GROUP_SIZE

    def step_rows(c, s):
        r0 = (c * CHUNK + s) * batch
        return slice(r0, r0 + batch)

    def gather(qs):
        for c0 in range(0, n_chunks, pass_chunks):
            prow = slice(c0 * batch, c0 * batch + pass_rows)
            sets = [[jnp.concatenate([u[step_rows(c0 + c, k * blocks + s), q * LANES:(q + 1) * LANES]
                                      for c in range(pass_chunks)], axis=0).astype(jnp.bfloat16)
                     for s in range(blocks)]
                    for q in qs for k in range(n_halves)]
            w = _block_transpose(sets, lane_block)
            for iq, q in enumerate(qs):
                for j in range(blocks):
                    lhs = jnp.concatenate([w[iq * n_halves + k][j] for k in range(n_halves)], axis=1)
                    lhs_ref[q * blocks + j, prow, :] = lhs

    def project(gs):
        for g in gs:
            sc_ref[g] = jnp.dot(lhs_ref[g], ws_ref[g], preferred_element_type=jnp.float32)
        for g in gs:
            recur(g)
        for g in gs:
            yg_ref[g] = (jnp.dot(lhs_ref[g], wt_ref[g], preferred_element_type=jnp.float32)
                         + jnp.dot(hp_ref[g], wy_ref[g], preferred_element_type=jnp.float32)
                         ).astype(jnp.bfloat16)

    def recur(g):
        a1 = a1_ref[g:g + 1, :]
        a2 = a2_ref[g:g + 1, :]
        sc = sc_ref[g, :, 0:2 * STATE_DIM]
        sc_sw = sc_ref[g, :, 2 * STATE_DIM:4 * STATE_DIM]
        h = hst_ref[g]
        hs = pltpu.roll(h, STATE_DIM, 1)
        h_prev = []
        for c in range(n_chunks):
            h_prev.append(h)
            rows_c = slice(c * batch, (c + 1) * batch)
            h, hs = a1 * h + a2 * hs + sc[rows_c], a1 * hs - a2 * h + sc_sw[rows_c]
        hst_ref[g] = h
        hp_ref[g] = jnp.concatenate(h_prev, axis=0).astype(jnp.bfloat16)

    def scatter(qs):
        for c0 in range(0, n_chunks, pass_chunks):
            prow = slice(c0 * batch, c0 * batch + pass_rows)
            sets = [[yg_ref[q * blocks + j, prow, k * LANES:(k + 1) * LANES] for j in range(blocks)]
                    for q in qs for k in range(n_halves)]
            z = _block_transpose(sets, lane_block)
            for iq, q in enumerate(qs):
                for k in range(n_halves):
                    for s in range(blocks):
                        zs = z[iq * n_halves + k][s].astype(jnp.float32)
                        for c in range(pass_chunks):
                            y_ref[step_rows(c0 + c, k * blocks + s), q * LANES:(q + 1) * LANES] = (
                                zs[c * batch:(c + 1) * batch])

    col_blocks = [list(range(q0, q0 + PIPE_COLS)) for q0 in range(0, n_cols, PIPE_COLS)]
    grp_blocks = [[q * blocks + j for q in qs for j in range(blocks)] for qs in col_blocks]
    n_blk = len(col_blocks)
    for step in range(n_blk + 2):
        if step < n_blk:
            gather(col_blocks[step])
        if 1 <= step <= n_blk:
            project(grp_blocks[step - 1])
        if step >= 2:
            scatter(col_blocks[step - 2])


def _s5_core_kernel(xp_ref, g_ref, ws_ref, kc_ref, wy_ref, a1_ref, a2_ref, y_ref, pre_ref, pim_ref, *rest,
                    n_tiles, batch, natural):
    if natural:
        xtm_ref, wt_ref, hst_ref, lhs_ref, sc_ref, hp_ref, yg_ref, slab_ref = rest
    else:
        wt_ref, hst_ref, lhs_ref, sc_ref, hp_ref, yg_ref = rest
    i = pl.program_id(0)

    @pl.when(i == 0)
    def _():
        hst_ref[...] = jnp.zeros_like(hst_ref)
        lane = lax.broadcasted_iota(jnp.int32, (GROUP_SIZE, MXU_K), 1)

        def build(g, carry):
            kc = kc_ref[g]
            for s in range(CHUNK):
                blk = kc if s == 0 else jnp.where(lane >= s * GROUP_SIZE, pltpu.roll(kc, s * GROUP_SIZE, 1), 0.0)
                wt_ref[g, s * GROUP_SIZE:(s + 1) * GROUP_SIZE, :] = blk.astype(jnp.bfloat16)
            return carry

        lax.fori_loop(0, N_GROUPS, build, 0)

    if natural:
        x = _to_time_major(xp_ref, slab_ref)
        xtm_ref[...] = x
    else:
        x = xp_ref[...]
    _s5_chunk_rows(x, hst_ref, g_ref, ws_ref, wt_ref, wy_ref, a1_ref, a2_ref,
                   y_ref, lhs_ref, sc_ref, hp_ref, yg_ref, batch=batch)

    @pl.when(i == n_tiles - 1)
    def _():
        low = lax.broadcasted_iota(jnp.int32, (batch, LANES), 1) < STATE_DIM
        for g in range(0, N_GROUPS, 2):
            h0, h1 = hst_ref[g], hst_ref[g + 1]
            cols = slice(g * STATE_DIM, (g + 2) * STATE_DIM)
            pre_ref[:, cols] = jnp.where(low, h0, pltpu.roll(h1, STATE_DIM, 1))
            pim_ref[:, cols] = jnp.where(low, pltpu.roll(h0, STATE_DIM, 1), h1)


def _s5_core(xp, gain, chunk_params, *, layer, mixer, batch, seq):
    natural = xp.ndim == 3
    ws, kc, wy, a1, a2 = chunk_params
    m = S5_TILE
    n_tiles = (seq * batch) // m

    xp_spec = _natural_spec(batch, m // batch, 0, n_tiles) if natural else _rows_spec(m, 0, n_tiles)
    rows_shape = jax.ShapeDtypeStruct((seq * batch, D_MODEL), jnp.float32)
    slab = [pltpu.VMEM((D_MODEL // LANES, m, LANES), jnp.float32)] if natural else []
    scratch = [
        pltpu.VMEM((N_GROUPS, MXU_K, MXU_K), jnp.bfloat16),
        pltpu.VMEM((N_GROUPS, batch, 2 * STATE_DIM), jnp.float32),
        pltpu.VMEM((N_GROUPS, m // CHUNK, MXU_K), jnp.bfloat16),
        pltpu.VMEM((N_GROUPS, m // CHUNK, 4 * STATE_DIM), jnp.float32),
        pltpu.VMEM((N_GROUPS, m // CHUNK, 2 * STATE_DIM), jnp.bfloat16),
        pltpu.VMEM((N_GROUPS, m // CHUNK, MXU_K), jnp.bfloat16),
    ]
    y, pre, pim, *maybe_rows = pl.pallas_call(
        functools.partial(_s5_core_kernel, n_tiles=n_tiles, batch=batch, natural=natural),
        grid=(n_tiles,),
        in_specs=[
            xp_spec,
            _const_spec((None, 1, D_MODEL), (layer, 0, 0)),
            _const_spec((None, N_GROUPS, MXU_K, 4 * STATE_DIM), (mixer, 0, 0, 0)),
            _const_spec((None, N_GROUPS, GROUP_SIZE, MXU_K), (mixer, 0, 0, 0)),
            _const_spec((None, N_GROUPS, 2 * STATE_DIM, MXU_K), (mixer, 0, 0, 0)),
            _const_spec((None, N_GROUPS, 2 * STATE_DIM), (mixer, 0, 0)),
            _const_spec((None, N_GROUPS, 2 * STATE_DIM), (mixer, 0, 0)),
        ],
        out_specs=[
            _rows_spec(m, 0, n_tiles),
            pl.BlockSpec((batch, N_STATE), lambda i: (0, 0)), pl.BlockSpec((batch, N_STATE), lambda i: (0, 0)),
        ] + ([_rows_spec(m, 0, n_tiles)] if natural else []),
        out_shape=[
            rows_shape,
            jax.ShapeDtypeStruct((batch, N_STATE), jnp.float32), jax.ShapeDtypeStruct((batch, N_STATE), jnp.float32),
        ] + ([rows_shape] if natural else []),
        scratch_shapes=scratch + slab,
        compiler_params=pltpu.CompilerParams(dimension_semantics=("arbitrary",), vmem_limit_bytes=VMEM_LIMIT),
        name="s5_core",
    )(xp, gain, ws, kc, wy, a1, a2)
    return y, pre, pim, (maybe_rows[0] if natural else xp)


def _s5_finish_rows(x, y_ref, g_ref, d_ref, wglu_ref):
    parts = []
    for r0 in range(0, x.shape[0], GLU_ROWS):
        rows = slice(r0, r0 + GLU_ROWS)
        ys = y_ref[rows, :] + d_ref[...] * _rms_norm(x[rows], g_ref[...])
        parts.append(_glu_tail(x[rows], jax.nn.gelu(ys).astype(jnp.bfloat16), wglu_ref))
    return jnp.concatenate(parts, axis=0)


def _conv_rows(x, g_ref, win_ref, cw_ref, wout_ref, cvx_ref, *, rows_per_step):
    br = rows_per_step
    m = x.shape[0]
    hist = HIST * br
    n_d = D_MODEL // W_CHUNK
    ub = _rms_norm(x, g_ref[...]).astype(jnp.bfloat16)
    bcv = [jnp.dot(ub, win_ref[c], preferred_element_type=jnp.float32) for c in range(3 * n_d)]
    outs = 0.0
    for c in range(n_d):
        cols = slice(c * W_CHUNK, (c + 1) * W_CHUNK)
        cv = bcv[n_d + c] * bcv[2 * n_d + c]
        cvx_ref[hist:hist + m, cols] = cv
        y = cw_ref[CONV_WIDTH - 1:CONV_WIDTH, cols] * cv
        for k in range(CONV_WIDTH - 1):
            y = y + cw_ref[k:k + 1, cols] * cvx_ref[k * br:k * br + m, cols]
        gated = (bcv[c] * y).astype(jnp.bfloat16)
        outs = outs + jnp.dot(gated, wout_ref[c], preferred_element_type=jnp.float32)
    cvx_ref[0:hist, :] = cvx_ref[m:m + hist, :]
    return x + outs


def _conv_kernel(xp_ref, xs_ref, bufs_ref, g_ref, win_f32_ref, cw_ref, wout_f32_ref,
                 op_ref, os_ref, nbufp_ref, nbufs_ref, win_ref, wout_ref, cvx_ref, *, n_cast, n_tiles, batch):
    i = pl.program_id(0)
    dec = xs_ref.shape[0]
    body = functools.partial(_conv_rows, g_ref=g_ref, win_ref=win_ref, cw_ref=cw_ref, wout_ref=wout_ref,
                             cvx_ref=cvx_ref)

    @pl.when(i < n_cast)
    def _():
        win_ref[i] = win_f32_ref[...].astype(jnp.bfloat16)

    @pl.when(i < wout_ref.shape[0])
    def _():
        wout_ref[i] = wout_f32_ref[...].astype(jnp.bfloat16)

    @pl.when(i == n_cast)
    def _():
        cvx_ref[0:HIST * batch, :] = jnp.zeros((HIST * batch, D_MODEL), jnp.float32)

    @pl.when(jnp.logical_and(i >= n_cast, i < n_cast + n_tiles))
    def _():
        op_ref[...] = body(xp_ref[...], rows_per_step=batch)

    @pl.when(i == n_cast + n_tiles - 1)
    def _():
        nbufp_ref[...] = cvx_ref[0:HIST * batch, :]

    @pl.when(i == n_cast + n_tiles)
    def _():
        cvx_ref[0:HIST * dec, :] = bufs_ref[...]
        os_ref[...] = body(xs_ref[...], rows_per_step=dec)
        nbufs_ref[...] = cvx_ref[0:HIST * dec, :]


def _conv_mixer(xp, xs, buf_s, gain, w_in, conv_w, w_out, *, layer, mixer, batch, seq):
    dec = xs.shape[0]
    m = CONV_TILE
    n_tiles = (seq * batch) // m
    n_cast = (3 * D_MODEL) // W_CHUNK
    n_out = D_MODEL // W_CHUNK
    kern = functools.partial(_conv_kernel, n_cast=n_cast, n_tiles=n_tiles, batch=batch)
    return pl.pallas_call(
        kern,
        grid=(n_cast + n_tiles + 1,),
        in_specs=[
            _rows_spec(m, n_cast, n_tiles),
            _const_spec((dec, D_MODEL)),
            _const_spec((None, HIST * dec, D_MODEL), (mixer, 0, 0)),
            _const_spec((None, 1, D_MODEL), (layer, 0, 0)),
            _col_chunk_spec(mixer, D_MODEL, n_cast),
            _const_spec((None, CONV_WIDTH, D_MODEL), (mixer, 0, 0)),
            _row_chunk_spec(mixer, D_MODEL, n_out),
        ],
        out_specs=[
            _rows_spec(m, n_cast, n_tiles),
            pl.BlockSpec((dec, D_MODEL), lambda i: (0, 0)),
            pl.BlockSpec((HIST * batch, D_MODEL), lambda i: (0, 0)),
            pl.BlockSpec((HIST * dec, D_MODEL), lambda i: (0, 0)),
        ],
        out_shape=[
            jax.ShapeDtypeStruct((seq * batch, D_MODEL), jnp.float32),
            jax.ShapeDtypeStruct((dec, D_MODEL), jnp.float32),
            jax.ShapeDtypeStruct((HIST * batch, D_MODEL), jnp.float32),
            jax.ShapeDtypeStruct((HIST * dec, D_MODEL), jnp.float32),
        ],
        scratch_shapes=[
            pltpu.VMEM((n_cast, D_MODEL, W_CHUNK), jnp.bfloat16),
            pltpu.VMEM((n_out, W_CHUNK, D_MODEL), jnp.bfloat16),
            pltpu.VMEM((max(HIST * batch + m, (HIST + 1) * dec), D_MODEL), jnp.float32),
        ],
        compiler_params=pltpu.CompilerParams(dimension_semantics=("arbitrary",), vmem_limit_bytes=VMEM_LIMIT),
        name="conv_mixer",
    )(xp, xs, buf_s, gain, w_in, conv_w, w_out)


def _mlp_chunk(ub, up_ref, down_ref, c):
    h = jnp.dot(ub, up_ref[c], preferred_element_type=jnp.float32)
    h = jnp.maximum(h, 0.0)
    h = (h * h).astype(jnp.bfloat16)
    return jnp.dot(h, down_ref[c], preferred_element_type=jnp.float32)


def _mlp_rows(x, g_ref, up_ref, down_ref, gf_ref, *, final_norm):
    ub = _rms_norm(x, g_ref[...]).astype(jnp.bfloat16)
    acc = x
    for c in range(D_FF // W_CHUNK):
        acc = acc + _mlp_chunk(ub, up_ref, down_ref, c)
    if final_norm:
        acc = _rms_norm(acc, gf_ref[...])
    return acc


def _mlp_kernel(xp_ref, xs_ref, g_ref, up_f32_ref, down_f32_ref, gf_ref, op_ref, os_ref,
                up_ref, down_ref, ub0_ref, acc0_ref, *maybe_slab_ref, n_cast, n_tiles, final_norm):
    i = pl.program_id(0)
    body = functools.partial(_mlp_rows, g_ref=g_ref, up_ref=up_ref, down_ref=down_ref, gf_ref=gf_ref,
                             final_norm=final_norm)

    def write_prompt(y):
        if maybe_slab_ref:
            _from_time_major(y, op_ref, maybe_slab_ref[0])
        else:
            op_ref[...] = y

    @pl.when(i == 0)
    def _():
        x = xp_ref[...]
        ub0_ref[...] = _rms_norm(x, g_ref[...]).astype(jnp.bfloat16)
        acc0_ref[...] = x

    @pl.when(i < n_cast)
    def _():
        up_ref[i] = up_f32_ref[...].astype(jnp.bfloat16)
        down_ref[i] = down_f32_ref[...].astype(jnp.bfloat16)
        acc0_ref[...] += _mlp_chunk(ub0_ref[...], up_ref, down_ref, i)

    @pl.when(i == n_cast - 1)
    def _():
        y = acc0_ref[...]
        write_prompt(_rms_norm(y, gf_ref[...]) if final_norm else y)

    @pl.when(jnp.logical_and(i >= n_cast, i < n_cast + n_tiles - 1))
    def _():
        write_prompt(body(xp_ref[...]))

    @pl.when(i == n_cast + n_tiles - 1)
    def _():
        os_ref[...] = body(xs_ref[...])


def _mlp(xp, xs, gain, w_up, w_down, gain_final, *, layer, batch, seq, final_norm):
    dec = xs.shape[0]
    m = ROW_TILE
    steps = m // batch
    n_tiles = (seq * batch) // m
    n_cast = D_FF // W_CHUNK
    scratch = [
        pltpu.VMEM((n_cast, D_MODEL, W_CHUNK), jnp.bfloat16),
        pltpu.VMEM((n_cast, W_CHUNK, D_MODEL), jnp.bfloat16),
        pltpu.VMEM((m, D_MODEL), jnp.bfloat16),
        pltpu.VMEM((m, D_MODEL), jnp.float32),
    ]
    first = n_cast - 1
    if final_norm:
        scratch.append(pltpu.VMEM((D_MODEL // LANES, m, LANES), jnp.float32))
        op_spec = _natural_spec(batch, steps, first, n_tiles)
        op_shape = jax.ShapeDtypeStruct((batch, seq, D_MODEL), jnp.float32)
    else:
        op_spec = _rows_spec(m, first, n_tiles)
        op_shape = jax.ShapeDtypeStruct((seq * batch, D_MODEL), jnp.float32)
    kern = functools.partial(_mlp_kernel, n_cast=n_cast, n_tiles=n_tiles, final_norm=final_norm)
    return pl.pallas_call(
        kern,
        grid=(n_cast + n_tiles,),
        in_specs=[
            _rows_spec(m, first, n_tiles),
            _const_spec((dec, D_MODEL)),
            _const_spec((None, 1, D_MODEL), (layer, 0, 0)),
            _col_chunk_spec(layer, D_MODEL, n_cast),
            _row_chunk_spec(layer, D_MODEL, n_cast),
            _const_spec((1, D_MODEL)),
        ],
        out_specs=[op_spec, pl.BlockSpec((dec, D_MODEL), lambda i: (0, 0))],
        out_shape=[op_shape, jax.ShapeDtypeStruct((dec, D_MODEL), jnp.float32)],
        scratch_shapes=scratch,
        compiler_params=pltpu.CompilerParams(dimension_semantics=("arbitrary",), vmem_limit_bytes=VMEM_LIMIT),
        name="mlp",
    )(xp, xs, gain, w_up, w_down, gain_final)


def _glu_mlp_kernel(xp_ref, y_ref, xs_ref, gmix_ref, d_ref, wglu_f32_ref, g_ref, up_f32_ref, down_f32_ref,
                    op_ref, os_ref, wglu_ref, up_ref, down_ref, ub0_ref, acc0_ref, *, n_pre, n_cast, n_tiles):
    i = pl.program_id(0)
    body = functools.partial(_mlp_rows, g_ref=g_ref, up_ref=up_ref, down_ref=down_ref, gf_ref=None,
                             final_norm=False)

    def mixed():
        return _s5_finish_rows(xp_ref[...], y_ref, gmix_ref, d_ref, wglu_ref)

    @pl.when(i < n_pre)
    def _():
        wglu_ref[i] = wglu_f32_ref[...].astype(jnp.bfloat16)

    @pl.when(i == n_pre)
    def _():
        x = mixed()
        ub0_ref[...] = _rms_norm(x, g_ref[...]).astype(jnp.bfloat16)
        acc0_ref[...] = x

    @pl.when(jnp.logical_and(i >= n_pre, i < n_pre + n_cast))
    def _():
        c = i - n_pre
        up_ref[c] = up_f32_ref[...].astype(jnp.bfloat16)
        down_ref[c] = down_f32_ref[...].astype(jnp.bfloat16)
        acc0_ref[...] += _mlp_chunk(ub0_ref[...], up_ref, down_ref, c)

    @pl.when(i == n_pre + n_cast - 1)
    def _():
        op_ref[...] = acc0_ref[...]

    @pl.when(jnp.logical_and(i >= n_pre + n_cast, i < n_pre + n_cast + n_tiles - 1))
    def _():
        op_ref[...] = body(mixed())

    @pl.when(i == n_pre + n_cast + n_tiles - 1)
    def _():
        os_ref[...] = body(xs_ref[...])


def _glu_mlp(xp, y, xs, gain_mix, d_skip, w_glu, gain, w_up, w_down, *, layer, mixer, batch, seq):
    dec = xs.shape[0]
    m = ROW_TILE
    n_tiles = (seq * batch) // m
    n_pre = (2 * D_MODEL) // W_CHUNK
    n_cast = D_FF // W_CHUNK
    first = n_pre + n_cast - 1
    return pl.pallas_call(
        functools.partial(_glu_mlp_kernel, n_pre=n_pre, n_cast=n_cast, n_tiles=n_tiles),
        grid=(n_pre + n_cast + n_tiles,),
        in_specs=[
            _rows_spec(m, first, n_tiles),
            _rows_spec(m, first, n_tiles),
            _const_spec((dec, D_MODEL)),
            _const_spec((None, 1, D_MODEL), (layer, 0, 0)),
            _const_spec((None, 1, D_MODEL), (mixer, 0, 0)),
            _col_chunk_spec(mixer, D_MODEL, n_pre),
            _const_spec((None, 1, D_MODEL), (layer, 0, 0)),
            pl.BlockSpec((None, D_MODEL, W_CHUNK), lambda i: (layer, 0, jnp.clip(i - n_pre, 0, n_cast - 1))),
            pl.BlockSpec((None, W_CHUNK, D_MODEL), lambda i: (layer, jnp.clip(i - n_pre, 0, n_cast - 1), 0)),
        ],
        out_specs=[_rows_spec(m, first, n_tiles), pl.BlockSpec((dec, D_MODEL), lambda i: (0, 0))],
        out_shape=[jax.ShapeDtypeStruct((seq * batch, D_MODEL), jnp.float32),
                   jax.ShapeDtypeStruct((dec, D_MODEL), jnp.float32)],
        scratch_shapes=[
            pltpu.VMEM((n_pre, D_MODEL, W_CHUNK), jnp.bfloat16),
            pltpu.VMEM((n_cast, D_MODEL, W_CHUNK), jnp.bfloat16),
            pltpu.VMEM((n_cast, W_CHUNK, D_MODEL), jnp.bfloat16),
            pltpu.VMEM((m, D_MODEL), jnp.bfloat16),
            pltpu.VMEM((m, D_MODEL), jnp.float32),
        ],
        compiler_params=pltpu.CompilerParams(dimension_semantics=("arbitrary",), vmem_limit_bytes=VMEM_LIMIT),
        name="glu_mlp",
    )(xp, y, xs, gain_mix, d_skip, w_glu, gain, w_up, w_down)


def _s5_discretise(a_re, a_im, log_dt, b_re, b_im):
    dt = jnp.exp(log_dt)[:, None]
    k = jnp.arange(CHUNK + 1, dtype=jnp.float32)[:, None, None]
    mag = jnp.exp(k * (a_re * dt))
    pw_re = mag * jnp.cos(k * (a_im * dt))
    pw_im = mag * jnp.sin(k * (a_im * dt))
    ab_re, ab_im = pw_re[1], pw_im[1]
    den = a_re * a_re + a_im * a_im
    q_re = ((ab_re - 1.0) * a_re + ab_im * a_im) / den
    q_im = (ab_im * a_re - (ab_re - 1.0) * a_im) / den
    bb_re = q_re[..., None] * b_re - q_im[..., None] * b_im
    bb_im = q_re[..., None] * b_im + q_im[..., None] * b_re
    return pw_re, pw_im, bb_re, bb_im


def _s5_step_params(pw_re, pw_im, bb_re, bb_im, c_re, c_im):
    gps = MXU_K // GROUP_SIZE
    eye = jnp.eye(gps, dtype=jnp.float32)

    def pack_b(w):
        w = w.reshape(2, N_SLABS, gps, STATE_DIM, GROUP_SIZE)
        return jnp.einsum('rsgph,gk->sghrkp', w, eye).reshape(N_SLABS, MXU_K, 2 * SLAB_STATE)

    def pack_c(w):
        w = w.reshape(2, N_SLABS, gps, GROUP_SIZE, STATE_DIM)
        return jnp.einsum('rsghp,gk->srgpkh', w, eye).reshape(N_SLABS, 2 * SLAB_STATE, MXU_K)

    wb = pack_b(jnp.stack([bb_re, bb_im])).astype(jnp.bfloat16)
    wc = pack_c(jnp.stack([c_re, -c_im])).astype(jnp.bfloat16)
    return pw_re[1].reshape(1, N_STATE), pw_im[1].reshape(1, N_STATE), wb, wc


def _s5_chunk_params(pw_re, pw_im, bb_re, bb_im, c_re, c_im):
    ps_re = jnp.transpose(pw_re[CHUNK - 1::-1], (1, 0, 2))[:, :, None, :]
    ps_im = jnp.transpose(pw_im[CHUNK - 1::-1], (1, 0, 2))[:, :, None, :]
    bt_re = jnp.transpose(bb_re, (0, 2, 1))[:, None]
    bt_im = jnp.transpose(bb_im, (0, 2, 1))[:, None]
    cat = jnp.concatenate
    ws = (cat([ps_re] * 4, -1) * cat([bt_re, bt_im, bt_im, bt_re], -1)
          + cat([-ps_im, ps_im, ps_im, -ps_im], -1) * cat([bt_im, bt_re, bt_re, bt_im], -1)
          ).reshape(N_GROUPS, MXU_K, 4 * STATE_DIM)
    pt_re = jnp.transpose(pw_re, (1, 2, 0))
    pt_im = jnp.transpose(pw_im, (1, 2, 0))
    ct_re = jnp.transpose(c_re, (0, 2, 1))[:, :, None, :]
    ct_im = jnp.transpose(c_im, (0, 2, 1))[:, :, None, :]

    def ca(t0, sign):
        p_r = pt_re[:, :, t0:t0 + CHUNK, None]
        p_i = pt_im[:, :, t0:t0 + CHUNK, None]
        return (cat([ct_re, sign * ct_im], 1) * cat([p_r, p_r], 1)
                + cat([-ct_im, sign * ct_re], 1) * cat([p_i, p_i], 1)).reshape(N_GROUPS, 2 * STATE_DIM, MXU_K)

    kc = jnp.einsum('gqh,gqx->ghx', cat([bb_re, -bb_im], 1), ca(0, 1.0), precision=lax.Precision.HIGH)
    wy = ca(1, -1.0)
    a1 = jnp.concatenate([pw_re[CHUNK], pw_re[CHUNK]], axis=-1)
    a2 = jnp.concatenate([-pw_im[CHUNK], pw_im[CHUNK]], axis=-1)
    return ws.astype(jnp.bfloat16), kc, wy.astype(jnp.bfloat16), a1, a2


def kernel(x_prompt, x_sample, state_ssm_re, state_ssm_im, state_conv, norm_mix, norm_mlp, norm_final,
           ssm_a_re, ssm_a_im, ssm_log_dt, ssm_b_re, ssm_b_im, ssm_c_re, ssm_c_im, ssm_d, ssm_w_glu,
           conv_w_in, conv_w, conv_w_out, mlp_w_up, mlp_w_down):
    batch, seq, _ = x_prompt.shape
    dec = x_sample.shape[0]
    n_ssm = ssm_a_re.shape[0]
    n_conv = conv_w_in.shape[0]

    g_mix = norm_mix.reshape(DEPTH, 1, D_MODEL)
    g_mlp = norm_mlp.reshape(DEPTH, 1, D_MODEL)
    g_final = norm_final.reshape(1, D_MODEL)
    d_skip = ssm_d.reshape(n_ssm, 1, D_MODEL)
    disc = jax.vmap(_s5_discretise)(ssm_a_re, ssm_a_im, ssm_log_dt, ssm_b_re, ssm_b_im)
    step_params = jax.vmap(_s5_step_params)(*disc, ssm_c_re, ssm_c_im)
    chunk_params = jax.vmap(_s5_chunk_params)(*disc, ssm_c_re, ssm_c_im)
    h0_re = state_ssm_re.reshape(n_ssm, dec, N_STATE)
    h0_im = state_ssm_im.reshape(n_ssm, dec, N_STATE)
    buf_s = jnp.transpose(state_conv, (0, 2, 1, 3)).reshape(n_conv, HIST * dec, D_MODEL)

    xp = x_prompt
    xs = x_sample.reshape(dec, D_MODEL)
    pre, pim, sre, sim, pbuf, sbuf = [], [], [], [], [], []
    for i in range(DEPTH):
        j = i // 2
        if i % 2 == 0:
            y, a, b, xp = _s5_core(xp, g_mix, chunk_params, layer=i, mixer=j, batch=batch, seq=seq)
            xs, c, d = _s5_sample(xs, h0_re, h0_im, g_mix, step_params, d_skip, ssm_w_glu, layer=i, mixer=j)
            pre.append(a), pim.append(b), sre.append(c), sim.append(d)
            xp, xs = _glu_mlp(xp, y, xs, g_mix, d_skip, ssm_w_glu, g_mlp, mlp_w_up, mlp_w_down,
                              layer=i, mixer=j, batch=batch, seq=seq)
        else:
            xp, xs, a, b = _conv_mixer(xp, xs, buf_s, g_mix, conv_w_in, conv_w, conv_w_out,
                                       layer=i, mixer=j, batch=batch, seq=seq)
            pbuf.append(a), sbuf.append(b)
            xp, xs = _mlp(xp, xs, g_mlp, mlp_w_up, mlp_w_down, g_final, layer=i, batch=batch, seq=seq,
                          final_norm=(i == DEPTH - 1))

    def states(parts, n):
        return jnp.stack(parts).reshape(n_ssm, n, N_GROUPS, STATE_DIM)

    def bufs(parts, n):
        return jnp.transpose(jnp.stack(parts).reshape(n_conv, HIST, n, D_MODEL), (0, 2, 1, 3))

    return (xp, xs.reshape(dec, 1, D_MODEL), states(pre, batch), states(pim, batch), bufs(pbuf, batch),
            states(sre, dec), states(sim, dec), bufs(sbuf, dec))
```
